```python
import jax, jax.numpy as jnp
from jax import lax
import numpy as np

D_MODEL = 1024
BATCH = 4
SEQ = 8192
DEPTH = 4

HEAD_DIM = 64
ATT_HEADS = 6
D_ATT = ATT_HEADS * HEAD_DIM
D_CONV = 256
CONV_WIDTH = 31
HGRN_HEADS = 6
D_HGRN = HGRN_HEADS * HEAD_DIM
D_MIX = D_ATT + D_CONV + D_HGRN
D_FF = -(-8 * D_MODEL // (3 * 256)) * 256
Q_BLOCK = 128
CHUNK = 64
IN_SIZES = (D_ATT, D_ATT, D_ATT, ATT_HEADS, 2 * D_CONV, D_HGRN, D_HGRN, D_HGRN, D_HGRN)
D_IN = D_ATT * 3 + ATT_HEADS + 2 * D_CONV + D_HGRN * 4
RMS_EPS = 1e-6
LN_EPS = 1e-5

kernel_name = "hybrid_fox_conformer_hgrn2_trunk"


def _split_points():
    pts, acc = [], 0
    for s in IN_SIZES[:-1]:
        acc += s
        pts.append(acc)
    return pts


def _rmsnorm(x, g):
    xf = x.astype(jnp.float32)
    y = xf * lax.rsqrt(jnp.mean(xf * xf, axis=-1, keepdims=True) + RMS_EPS)
    return (y * g.astype(jnp.float32)).astype(x.dtype)


def _layernorm(x, g, b):
    xf = x.astype(jnp.float32)
    mu = jnp.mean(xf, axis=-1, keepdims=True)
    xc = xf - mu
    var = jnp.mean(xc * xc, axis=-1, keepdims=True)
    y = xc * lax.rsqrt(var + LN_EPS)
    return (y * g.astype(jnp.float32) + b.astype(jnp.float32)).astype(x.dtype)


def _heads(t, n):
    B, T, _ = t.shape
    return t.reshape(B, T, n, -1).transpose(0, 2, 1, 3)


def _merge(t):
    B, H, T, Dh = t.shape
    return t.transpose(0, 2, 1, 3).reshape(B, T, H * Dh)


def forgetting_attention(q, k, v, log_f):
    B, H, T, Dh = q.shape
    nb = T // Q_BLOCK
    c = jnp.cumsum(log_f.astype(jnp.float32), axis=-1)
    qb = q.reshape(B, H, nb, Q_BLOCK, Dh).transpose(2, 0, 1, 3, 4)
    cb = c.reshape(B, H, nb, Q_BLOCK).transpose(2, 0, 1, 3)
    kpos = jnp.arange(T)
    scale = HEAD_DIM ** -0.5

    def one_block(args):
        i, q_i, c_i = args
        s = jnp.einsum('bhqd,bhkd->bhqk', q_i, k).astype(jnp.float32) * scale
        s = s + (c_i[..., :, None] - c[..., None, :])
        qpos = i * Q_BLOCK + jnp.arange(Q_BLOCK)
        causal = kpos[None, :] <= qpos[:, None]
        s = jnp.where(causal, s, -jnp.inf)
        p = jax.nn.softmax(s, axis=-1).astype(v.dtype)
        return jnp.einsum('bhqk,bhkd->bhqd', p, v)

    o = lax.map(one_block, (jnp.arange(nb), qb, cb))
    return o.transpose(1, 2, 0, 3, 4).reshape(B, H, T, Dh)


def conformer_conv(u, conv_w, conv_b, ln_g, ln_b):
    a, gate = jnp.split(u, 2, axis=-1)
    h = a * jax.nn.sigmoid(gate)
    h = lax.conv_general_dilated(
        h, conv_w[:, None, :], window_strides=(1,), padding=[(CONV_WIDTH - 1, 0)],
        dimension_numbers=('NWC', 'WIO', 'NWC'), feature_group_count=D_CONV) + conv_b
    h = _layernorm(h, ln_g, ln_b)
    return jax.nn.silu(h)


def hgrn2_recurrence(q, k, v, log_f):
    B, H, T, Dk = q.shape
    Dv = v.shape[-1]
    nc = T // CHUNK

    def to_chunks(t):
        return t.reshape(B, H, nc, CHUNK, t.shape[-1]).transpose(2, 0, 1, 3, 4)

    tri = jnp.tril(jnp.ones((CHUNK, CHUNK), dtype=bool))

    def step(S, inp):
        q_c, k_c, v_c, g_c = inp
        b = jnp.cumsum(g_c, axis=-2)
        diff = b[..., :, None, :] - b[..., None, :, :]
        decay = jnp.exp(jnp.where(tri[:, :, None], diff, -jnp.inf))
        a = jnp.einsum('bhtd,bhtsd,bhsd->bhts', q_c, decay, k_c)
        o = (jnp.einsum('bhts,bhsv->bhtv', a, v_c)
             + jnp.einsum('bhtd,bhdv->bhtv', q_c * jnp.exp(b), S))
        b_end = b[..., -1:, :]
        S = (jnp.exp(b_end[..., 0, :])[..., None] * S
             + jnp.einsum('bhsd,bhsv->bhdv', k_c * jnp.exp(b_end - b), v_c))
        return S, o

    S0 = jnp.zeros((B, H, Dk, Dv), jnp.float32)
    _, o = lax.scan(step, S0, (to_chunks(q), to_chunks(k), to_chunks(v), to_chunks(log_f)))
    return o.transpose(1, 2, 0, 3, 4).reshape(B, H, T, Dv)


def setup_inputs(seed: int = 0) -> dict:
    key = jax.random.key(seed)
    ks = jax.random.split(key, 16)
    f32 = jnp.float32

    def nrm(k, shape, scale):
        return jax.random.normal(k, shape, f32) * scale

    res_scale = (2 * DEPTH) ** -0.5
    return {
        "x": nrm(ks[0], (BATCH, SEQ, D_MODEL), 1.0),
        "norm_mix_g": 1.0 + nrm(ks[1], (DEPTH, D_MODEL), 0.02),
        "w_in": nrm(ks[2], (DEPTH, D_MODEL, D_IN), D_MODEL ** -0.5),
        "fgate_b": 2.0 + nrm(ks[3], (DEPTH, ATT_HEADS), 0.1),
        "conv_w": nrm(ks[4], (DEPTH, CONV_WIDTH, D_CONV), CONV_WIDTH ** -0.5),
        "conv_b": nrm(ks[5], (DEPTH, D_CONV), 0.02),
        "conv_ln_g": 1.0 + nrm(ks[6], (DEPTH, D_CONV), 0.02),
        "conv_ln_b": nrm(ks[7], (DEPTH, D_CONV), 0.02),
        "hgrn_lb_logits": nrm(ks[8], (DEPTH, D_HGRN), 0.5),
        "hgrn_norm_g": 1.0 + nrm(ks[9], (DEPTH, HEAD_DIM), 0.02),
        "w_out": nrm(ks[10], (DEPTH, D_MIX, D_MODEL), D_MIX ** -0.5 * res_scale),
        "norm_ffn_g": 1.0 + nrm(ks[11], (DEPTH, D_MODEL), 0.02),
        "w_ffn_in": nrm(ks[12], (DEPTH, D_MODEL, 2 * D_FF), D_MODEL ** -0.5),
        "w_ffn_out": nrm(ks[13], (DEPTH, D_FF, D_MODEL), D_FF ** -0.5 * res_scale),
        "norm_final_g": 1.0 + nrm(ks[14], (D_MODEL,), 0.02),
    }


def reference(x, norm_mix_g, w_in, fgate_b, conv_w, conv_b, conv_ln_g, conv_ln_b,
              hgrn_lb_logits, hgrn_norm_g, w_out, norm_ffn_g, w_ffn_in, w_ffn_out, norm_final_g):
    f32 = jnp.float32
    lb_all = jnp.cumsum(jax.nn.softmax(hgrn_lb_logits.astype(f32), axis=0), axis=0)
    lb_all = lb_all - lb_all[0:1]
    split_pts = _split_points()

    for l in range(DEPTH):
        h = _rmsnorm(x, norm_mix_g[l])
        z = h @ w_in[l]
        q_a, k_a, v_a, f_a, c_in, q_h, f_h, i_h, g_h = jnp.split(z, split_pts, axis=-1)

        log_fa = jax.nn.log_sigmoid(f_a.astype(f32) + fgate_b[l].astype(f32)).transpose(0, 2, 1)
        att = _merge(forgetting_attention(_heads(q_a, ATT_HEADS), _heads(k_a, ATT_HEADS),
                                          _heads(v_a, ATT_HEADS), log_fa))

        cnv = conformer_conv(c_in, conv_w[l], conv_b[l], conv_ln_g[l], conv_ln_b[l])

        lb = lb_all[l]
        f = lb + (1.0 - lb) * jax.nn.sigmoid(f_h.astype(f32))
        o_h = hgrn2_recurrence(_heads(q_h.astype(f32), HGRN_HEADS), _heads(1.0 - f, HGRN_HEADS),
                               _heads(i_h.astype(f32), HGRN_HEADS), _heads(jnp.log(f), HGRN_HEADS))
        o_h = _rmsnorm(o_h, hgrn_norm_g[l])
        hg = _merge(o_h).astype(x.dtype) * jax.nn.silu(g_h)

        mix = jnp.concatenate([att.astype(x.dtype), cnv.astype(x.dtype), hg], axis=-1) @ w_out[l]
        x = x + mix

        h = _rmsnorm(x, norm_ffn_g[l])
        gate, up = jnp.split(h @ w_ffn_in[l], 2, axis=-1)
        x = x + (jax.nn.silu(gate) * up) @ w_ffn_out[l]

    return _rmsnorm(x, norm_final_g)
```

```python
import functools

import numpy as np
import jax
import jax.numpy as jnp
from jax import lax
from jax.experimental import pallas as pl
from jax.experimental.pallas import tpu as pltpu

F32 = jnp.float32
BF16 = jnp.bfloat16

HEAD_DIM = 64
ATT_HEADS = 6
D_ATT = ATT_HEADS * HEAD_DIM
D_CONV = 256
CONV_WIDTH = 31
HGRN_HEADS = 6
D_HGRN = HGRN_HEADS * HEAD_DIM
RMS_EPS = 1e-6
LN_EPS = 1e-5

LANES = 128
HEAD_PAIRS = D_ATT // LANES
CONV_HALO = 32
VMEM_LIMIT_BYTES = 56 * 1024 * 1024

_IN_COLS = (("qa", D_ATT), ("ka", D_ATT), ("va", D_ATT), ("fa", D_ATT), ("cin", 2 * D_CONV),
            ("qh", D_HGRN), ("fh", D_HGRN), ("ih", D_HGRN), ("gh", D_HGRN))
_IN_DTYPES = {"qa": BF16, "ka": BF16, "va": BF16, "fa": F32, "cin": F32,
              "qh": F32, "fh": F32, "ih": BF16, "gh": F32}
D_IN_PAD = sum(n for _, n in _IN_COLS)


def _nt_dot(a, b):
    return lax.dot_general(a, b, (((1,), (1,)), ((), ())), preferred_element_type=F32)


def _dot(a, b):
    return jnp.dot(a, b, preferred_element_type=F32)


def _rmsnorm_rows(x, g):
    return x * lax.rsqrt(jnp.mean(x * x, axis=-1, keepdims=True) + RMS_EPS) * g


def _compiler_params(semantics):
    return pltpu.CompilerParams(dimension_semantics=semantics, vmem_limit_bytes=VMEM_LIMIT_BYTES)


def _resident(shape):
    nd = len(shape)
    return pl.BlockSpec(shape, lambda *_: (0,) * nd, pipeline_mode=pl.Buffered(1))


def _inproj_kernel(x_ref, g_ref, w_ref, fb_ref, *out_refs):
    h = _rmsnorm_rows(x_ref[...], g_ref[...]).astype(BF16)
    off = 0
    for (name, n), o_ref in zip(_IN_COLS, out_refs):
        z = _dot(h, w_ref[:, off:off + n])
        if name == "qa":
            z = z * (HEAD_DIM ** -0.5)
        elif name == "fa":
            z = jax.nn.log_sigmoid(z + fb_ref[...])
        o_ref[...] = z.astype(o_ref.dtype)
        off += n


def _inproj(x2d, g, w, fb, tm):
    m, d = x2d.shape
    out_shape = [jax.ShapeDtypeStruct((m, n), _IN_DTYPES[name]) for name, n in _IN_COLS]
    out_specs = [pl.BlockSpec((tm, n), lambda i: (i, 0)) for _, n in _IN_COLS]
    return pl.pallas_call(
        _inproj_kernel,
        grid=(m // tm,),
        in_specs=[pl.BlockSpec((tm, d), lambda i: (i, 0)), _resident((1, d)),
                  _resident(w.shape), _resident(fb.shape)],
        out_specs=out_specs,
        out_shape=out_shape,
        compiler_params=_compiler_params(("parallel",)),
        name="inproj",
    )(x2d, g, w, fb)


def _split3(x):
    x1 = x.astype(BF16)
    r1 = x - x1.astype(F32)
    x2 = r1.astype(BF16)
    x3 = (r1 - x2.astype(F32)).astype(BF16)
    return x1, x2, x3


def _cumsum_kernel(ls_ref, tri_ref, c_ref, crow_ref, carry_ref):
    @pl.when(pl.program_id(1) == 0)
    def _():
        carry_ref[...] = jnp.zeros_like(carry_ref)

    tri = tri_ref[...]
    x1, x2, x3 = _split3(ls_ref[0])
    c = _dot(tri, x1) + _dot(tri, x2) + _dot(tri, x3) + carry_ref[...]
    c_ref[0] = c
    rows = c.shape[0]
    carry_ref[...] = c[rows - 1:rows, :]
    for p in range(HEAD_PAIRS):
        ct = c[:, p * LANES:(p + 1) * LANES].T
        crow_ref[0, p] = jnp.concatenate([ct[0:1], ct[HEAD_DIM:HEAD_DIM + 1]], axis=0)


def _forget_cumsum(ls, tri):
    b, t, d = ls.shape
    cs = tri.shape[0]
    return pl.pallas_call(
        _cumsum_kernel,
        grid=(b, t // cs),
        in_specs=[pl.BlockSpec((1, cs, d), lambda i, j: (i, j, 0)), _resident(tri.shape)],
        out_specs=[pl.BlockSpec((1, cs, d), lambda i, j: (i, j, 0)),
                   pl.BlockSpec((1, HEAD_PAIRS, 2, cs), lambda i, j: (i, 0, 0, j))],
        out_shape=[jax.ShapeDtypeStruct((b, t, d), F32),
                   jax.ShapeDtypeStruct((b, HEAD_PAIRS, 2, t), F32)],
        scratch_shapes=[pltpu.VMEM((1, d), F32)],
        compiler_params=_compiler_params(("parallel", "arbitrary")),
        name="forget_cumsum",
    )(ls, tri)


_FIRST, _LAST, _MASKED = 1, 2, 4


def _causal_tables(t, tq, tk):
    ii, jj, fl = [], [], []
    for i in range(t // tq):
        nj = -(-((i + 1) * tq) // tk)
        for j in range(nj):
            flag = (_FIRST if j == 0 else 0) | (_LAST if j == nj - 1 else 0)
            if (j + 1) * tk - 1 > i * tq:
                flag |= _MASKED
            ii.append(i), jj.append(j), fl.append(flag)
    return (np.asarray(ii, np.int32), np.asarray(jj, np.int32), np.asarray(fl, np.int32))


def _attn_kernel(ii_ref, jj_ref, fl_ref, q_ref, k_ref, v_ref, cq_ref, ck_ref, o_ref,
                 m_ref, l_ref, acc_ref, *, tq, tk):
    step = pl.program_id(2)
    i, j, flag = ii_ref[step], jj_ref[step], fl_ref[step]
    lane = lax.broadcasted_iota(jnp.int32, (1, LANES), 1)

    @pl.when((flag & _FIRST) != 0)
    def _():
        m_ref[...] = jnp.full_like(m_ref, -jnp.inf)
        l_ref[...] = jnp.zeros_like(l_ref)
        acc_ref[...] = jnp.zeros_like(acc_ref)

    def block(masked):
        q2, k2, v2 = q_ref[0], k_ref[0], v_ref[0]
        cq2, ck2 = cq_ref[0], ck_ref[0, 0]
        if masked:
            row = i * tq + lax.broadcasted_iota(jnp.int32, (tq, tk), 0)
            col = j * tk + lax.broadcasted_iota(jnp.int32, (tq, tk), 1)
            keep = col <= row
        for hh in range(2):
            in_head = (lane >= hh * HEAD_DIM) & (lane < (hh + 1) * HEAD_DIM)
            qm = jnp.where(in_head, q2, jnp.zeros_like(q2))
            s = _nt_dot(qm, k2)
            s = s + cq2[:, hh * HEAD_DIM:hh * HEAD_DIM + 1] - ck2[hh:hh + 1, :]
            if masked:
                s = jnp.where(keep, s, -jnp.inf)
            m_prev = m_ref[hh]
            m_new = jnp.maximum(m_prev, jnp.max(s, axis=-1, keepdims=True))
            alpha = jnp.exp(m_prev - m_new)
            p = jnp.exp(s - m_new)
            l_ref[hh] = alpha * l_ref[hh] + jnp.sum(p, axis=-1, keepdims=True)
            acc_ref[hh] = alpha * acc_ref[hh] + _dot(p.astype(BF16), v2)
            m_ref[hh] = m_new

    @pl.when((flag & _MASKED) != 0)
    def _():
        block(True)

    @pl.when((flag & _MASKED) == 0)
    def _():
        block(False)

    @pl.when((flag & _LAST) != 0)
    def _():
        o0 = acc_ref[0] / l_ref[0]
        o1 = acc_ref[1] / l_ref[1]
        o_ref[0] = jnp.where(lane < HEAD_DIM, o0, o1).astype(o_ref.dtype)


def _attention(qa, ka, va, c3, crow, tq, tk):
    b, t, _ = qa.shape
    ii, jj, fl = _causal_tables(t, tq, tk)
    q_map = lambda bb, p, s, ii, jj, fl: (bb, ii[s], p)
    k_map = lambda bb, p, s, ii, jj, fl: (bb, jj[s], p)
    grid_spec = pltpu.PrefetchScalarGridSpec(
        num_scalar_prefetch=3,
        grid=(b, HEAD_PAIRS, len(ii)),
        in_specs=[pl.BlockSpec((1, tq, LANES), q_map),
                  pl.BlockSpec((1, tk, LANES), k_map),
                  pl.BlockSpec((1, tk, LANES), k_map),
                  pl.BlockSpec((1, tq, LANES), q_map),
                  pl.BlockSpec((1, 1, 2, tk), lambda bb, p, s, ii, jj, fl: (bb, p, 0, jj[s]))],
        out_specs=pl.BlockSpec((1, tq, LANES), q_map),
        scratch_shapes=[pltpu.VMEM((2, tq, 1), F32), pltpu.VMEM((2, tq, 1), F32),
                        pltpu.VMEM((2, tq, LANES), F32)],
    )
    return pl.pallas_call(
        functools.partial(_attn_kernel, tq=tq, tk=tk),
        grid_spec=grid_spec,
        out_shape=jax.ShapeDtypeStruct((b, t, D_ATT), BF16),
        compiler_params=_compiler_params(("parallel", "parallel", "arbitrary")),
        name="fox_attention",
    )(jnp.asarray(ii), jnp.asarray(jj), jnp.asarray(fl), qa, ka, va, c3, crow)


_CONV_ROWS = 64


def _conv_kernel(cur_ref, halo_ref, w_ref, b_ref, lg_ref, lb_ref, o_ref, hp_ref, *, tc):
    def glu(u):
        return u[:, :D_CONV] * jax.nn.sigmoid(u[:, D_CONV:])

    hp_ref[CONV_HALO:, :] = glu(cur_ref[0])
    halo = glu(halo_ref[0])
    hp_ref[0:CONV_HALO, :] = jnp.where(pl.program_id(1) > 0, halo, jnp.zeros_like(halo))
    first = CONV_HALO - (CONV_WIDTH - 1)
    for r0 in range(0, tc, _CONV_ROWS):
        acc = jnp.zeros((_CONV_ROWS, D_CONV), F32)
        for tap in range(CONV_WIDTH):
            acc = acc + hp_ref[r0 + first + tap:r0 + first + tap + _CONV_ROWS, :] * w_ref[tap:tap + 1, :]
        acc = acc + b_ref[...]
        mu = jnp.mean(acc, axis=-1, keepdims=True)
        xc = acc - mu
        var = jnp.mean(xc * xc, axis=-1, keepdims=True)
        y = xc * lax.rsqrt(var + LN_EPS) * lg_ref[...] + lb_ref[...]
        o_ref[0, r0:r0 + _CONV_ROWS, :] = (y * jax.nn.sigmoid(y)).astype(o_ref.dtype)


def _conv_module(cin, w, bias, ln_g, ln_b, tc):
    b, t, d = cin.shape
    halo_blocks = tc // CONV_HALO
    return pl.pallas_call(
        functools.partial(_conv_kernel, tc=tc),
        grid=(b, t // tc),
        in_specs=[pl.BlockSpec((1, tc, d), lambda i, j: (i, j, 0)),
                  pl.BlockSpec((1, CONV_HALO, d), lambda i, j: (i, jnp.maximum(j * halo_blocks - 1, 0), 0)),
                  _resident(w.shape), _resident(bias.shape), _resident(ln_g.shape), _resident(ln_b.shape)],
        out_specs=pl.BlockSpec((1, tc, D_CONV), lambda i, j: (i, j, 0)),
        out_shape=jax.ShapeDtypeStruct((b, t, D_CONV), BF16),
        scratch_shapes=[pltpu.VMEM((tc + CONV_HALO, D_CONV), F32)],
        compiler_params=_compiler_params(("parallel", "parallel")),
        name="conformer_conv",
    )(cin, cin, w, bias, ln_g, ln_b)


HGRN_CHUNK = 128
_HGRN_LEVELS = HGRN_CHUNK.bit_length()


def _hgrn_constants():
    c = HGRN_CHUNK
    t = np.arange(c)[:, None]
    s = np.arange(c)[None, :]
    mats = [(s <= t), (s > t)]
    for lvl in range(2, _HGRN_LEVELS):
        same = (s >> (lvl - 1)) == (t >> (lvl - 1))
        mats += [(s <= t) & same, (s > t) & same]
    stack = np.concatenate([m.astype(np.float32) for m in mats], axis=0)
    x = t ^ s
    level = np.where(t == s, 0, np.where(s < t, np.floor(np.log2(np.maximum(x, 1))).astype(np.int32) + 1, -1))
    return stack, level.astype(np.int32)


def _hgrn_kernel(q_ref, f_ref, v_ref, g_ref, lbl_ref, gn_ref, cm_ref, lvl_ref, o_ref, st_ref,
                 *, layer, chunks):
    c = HGRN_CHUNK

    @pl.when(pl.program_id(2) == 0)
    def _():
        st_ref[...] = jnp.zeros_like(st_ref)

    z = lbl_ref[...]
    e = jnp.exp(z - jnp.max(z, axis=0, keepdims=True))
    if layer > 0:
        lb = jnp.sum(e[1:layer + 1], axis=0, keepdims=True) / jnp.sum(e, axis=0, keepdims=True)
    else:
        lb = jnp.zeros((1, LANES), F32)

    lane = lax.broadcasted_iota(jnp.int32, (1, LANES), 1)
    head0 = lane < HEAD_DIM
    level = lvl_ref[...]
    vi = lax.broadcasted_iota(jnp.int32, (LANES, LANES), 0) // HEAD_DIM
    di = lax.broadcasted_iota(jnp.int32, (LANES, LANES), 1) // HEAD_DIM
    same_head = vi == di

    for ch in range(chunks):
        rows = slice(ch * c, (ch + 1) * c)
        q = q_ref[0, rows, :]
        v2 = v_ref[0, rows, :]
        f = lb + (1.0 - lb) * jax.nn.sigmoid(f_ref[0, rows, :])
        kk = 1.0 - f
        g = jnp.log(f)
        g1 = g.astype(BF16)
        g2 = (g - g1.astype(F32)).astype(BF16)
        sums = _dot(cm_ref[...], jnp.concatenate([g1, g2], axis=1))
        sums = sums[:, :LANES] + sums[:, LANES:]

        def seg(idx):
            return sums[idx * c:(idx + 1) * c, :]

        b_incl, b_rest = seg(0), seg(1)
        k_b = kk.astype(BF16)
        a = [jnp.zeros((c, c), F32), jnp.zeros((c, c), F32)]
        for lvl in range(_HGRN_LEVELS):
            if lvl == 0:
                qt, kt = q, k_b
            elif lvl == 1:
                qt, kt = q * jnp.exp(g), k_b
            else:
                qt = q * jnp.exp(seg(2 * (lvl - 1)))
                kt = (kk * jnp.exp(seg(2 * (lvl - 1) + 1))).astype(BF16)
            for hh in range(2):
                qm = jnp.where(head0 if hh == 0 else ~head0, qt, 0.0).astype(BF16)
                a[hh] = jnp.where(level == lvl, _nt_dot(qm, kt), a[hh])
        o = jnp.where(head0, _dot(a[0].astype(BF16), v2), _dot(a[1].astype(BF16), v2))
        st = st_ref[...]
        o = o + _nt_dot((q * jnp.exp(b_incl)).astype(BF16), st.astype(BF16))
        v_t = v2.astype(F32).T.astype(BF16)
        upd = _dot(v_t, (kk * jnp.exp(b_rest)).astype(BF16))
        st_ref[...] = st * jnp.exp(b_incl[c - 1:c, :]) + jnp.where(same_head, upd, 0.0)

        o2 = o * o
        ms0 = jnp.sum(jnp.where(head0, o2, 0.0), axis=-1, keepdims=True) * (1.0 / HEAD_DIM)
        ms1 = jnp.sum(jnp.where(head0, 0.0, o2), axis=-1, keepdims=True) * (1.0 / HEAD_DIM)
        inv = jnp.where(head0, lax.rsqrt(ms0 + RMS_EPS), lax.rsqrt(ms1 + RMS_EPS))
        gate = g_ref[0, rows, :]
        o_ref[0, rows, :] = (o * inv * gn_ref[...] * (gate * jax.nn.sigmoid(gate))).astype(o_ref.dtype)


def _hgrn(qh, fh, ih, gh, lb_logits, gn, cmat, lvl, layer, tt):
    b, t, _ = qh.shape
    tile = pl.BlockSpec((1, tt, LANES), lambda i, p, j: (i, j, p))
    return pl.pallas_call(
        functools.partial(_hgrn_kernel, layer=layer, chunks=tt // HGRN_CHUNK),
        grid=(b, HEAD_PAIRS, t // tt),
        in_specs=[tile, tile, tile, tile,
                  pl.BlockSpec((lb_logits.shape[0], LANES), lambda i, p, j: (0, p)),
                  _resident(gn.shape), _resident(cmat.shape), _resident(lvl.shape)],
        out_specs=tile,
        out_shape=jax.ShapeDtypeStruct((b, t, D_HGRN), BF16),
        scratch_shapes=[pltpu.VMEM((LANES, LANES), F32)],
        compiler_params=_compiler_params(("parallel", "parallel", "arbitrary")),
        name="hgrn2",
    )(qh, fh, ih, gh, lb_logits, gn, cmat, lvl)


_FFN_COLS = 256


def _ffn_kernel(x_ref, att_ref, cnv_ref, hg_ref, wo_ref, g_ref, wi_ref, w2_ref, gf_ref, o_ref, a_ref,
                *, d_ff, final):
    mix = jnp.concatenate([att_ref[...], cnv_ref[...], hg_ref[...]], axis=1)
    x1 = x_ref[...] + _dot(mix, wo_ref[...])
    h = _rmsnorm_rows(x1, g_ref[...]).astype(BF16)
    for c0 in range(0, d_ff, _FFN_COLS):
        gate = _dot(h, wi_ref[:, c0:c0 + _FFN_COLS])
        up = _dot(h, wi_ref[:, d_ff + c0:d_ff + c0 + _FFN_COLS])
        a_ref[:, c0:c0 + _FFN_COLS] = (gate * jax.nn.sigmoid(gate) * up).astype(BF16)
    y = x1 + _dot(a_ref[...], w2_ref[...])
    if final:
        y = _rmsnorm_rows(y, gf_ref[...])
    o_ref[...] = y


def _out_ffn(x2d, att, cnv, hg, wo, g, wi, w2, gf, tm, final):
    m, d = x2d.shape
    d_ff = w2.shape[0]
    row = lambda n: pl.BlockSpec((tm, n), lambda i: (i, 0))
    return pl.pallas_call(
        functools.partial(_ffn_kernel, d_ff=d_ff, final=final),
        grid=(m // tm,),
        in_specs=[row(d), row(D_ATT), row(D_CONV), row(D_HGRN), _resident(wo.shape), _resident(g.shape),
                  _resident(wi.shape), _resident(w2.shape), _resident(gf.shape)],
        out_specs=row(d),
        out_shape=jax.ShapeDtypeStruct((m, d), F32),
        scratch_shapes=[pltpu.VMEM((tm, d_ff), BF16)],
        compiler_params=_compiler_params(("parallel",)),
        name="out_ffn",
    )(x2d, att, cnv, hg, wo, g, wi, w2, gf)


def _tiles(t):
    return dict(tm=min(512, t), cs=min(256, t), tq=min(512, t), tk=min(512, t),
                tc=min(512, t), tt=min(256, t))


def _rearranged_w_in(w_in_l):
    sizes = (D_ATT, D_ATT, D_ATT, ATT_HEADS, 2 * D_CONV, D_HGRN, D_HGRN, D_HGRN, D_HGRN)
    pts = np.cumsum(sizes)[:-1].tolist()
    q, k, v, f, cin, qh, fh, ih, gh = jnp.split(w_in_l, pts, axis=-1)
    f = jnp.repeat(f, HEAD_DIM, axis=-1)
    return jnp.concatenate([q, k, v, f, cin, qh, fh, ih, gh], axis=-1).astype(BF16)


def kernel(x, norm_mix_g, w_in, fgate_b, conv_w, conv_b, conv_ln_g, conv_ln_b, hgrn_lb_logits, hgrn_norm_g,
           w_out, norm_ffn_g, w_ffn_in, w_ffn_out, norm_final_g):
    b, t, d = x.shape
    depth = w_in.shape[0]
    tl = _tiles(t)
    row = lambda a: a.reshape(1, -1).astype(F32)

    tri = jnp.asarray(np.tril(np.ones((tl["cs"], tl["cs"]), np.float32)), BF16)
    cstack, lvl = _hgrn_constants()
    cmat = jnp.asarray(cstack, BF16)
    lvl = jnp.asarray(lvl)
    lb_logits = hgrn_lb_logits.astype(F32)
    gf = row(norm_final_g)

    x2d = x.reshape(b * t, d)
    for l in range(depth):
        w = _rearranged_w_in(w_in[l])
        fb = row(jnp.repeat(fgate_b[l], HEAD_DIM))
        outs = _inproj(x2d, row(norm_mix_g[l]), w, fb, tl["tm"])
        qa, ka, va, ls, cin, qh, fh, ih, gh = [o.reshape(b, t, -1) for o in outs]

        c3, crow = _forget_cumsum(ls, tri)
        att = _attention(qa, ka, va, c3, crow, tl["tq"], tl["tk"])

        cw = jnp.concatenate([conv_w[l], jnp.zeros((1, D_CONV), F32)], axis=0)
        cnv = _conv_module(cin, cw, row(conv_b[l]), row(conv_ln_g[l]), row(conv_ln_b[l]), tl["tc"])

        gn = row(jnp.tile(hgrn_norm_g[l], LANES // HEAD_DIM))
        hg = _hgrn(qh, fh, ih, gh, lb_logits, gn, cmat, lvl, l, tl["tt"])

        x2d = _out_ffn(x2d, att.reshape(b * t, -1), cnv.reshape(b * t, -1), hg.reshape(b * t, -1),
                       w_out[l].astype(BF16), row(norm_ffn_g[l]), w_ffn_in[l].astype(BF16),
                       w_ffn_out[l].astype(BF16), gf, tl["tm"], final=(l == depth - 1))
    return x2d.reshape(b, t, d)
```

```python
import functools

import numpy as np
import jax
import jax.numpy as jnp
from jax import lax
from jax.experimental import pallas as pl
from jax.experimental.pallas import tpu as pltpu

F32 = jnp.float32
BF16 = jnp.bfloat16

HEAD_DIM = 64
ATT_HEADS = 6
D_ATT = ATT_HEADS * HEAD_DIM
D_CONV = 256
CONV_WIDTH = 31
HGRN_HEADS = 6
D_HGRN = HGRN_HEADS * HEAD_DIM
RMS_EPS = 1e-6
LN_EPS = 1e-5

LANES = 128
HEAD_PAIRS = D_ATT // LANES
CONV_HALO = 32
VMEM_LIMIT_BYTES = 56 * 1024 * 1024

D_ATT_X = ATT_HEADS * LANES
LOG2E = 1.4426950408889634

_IN_COLS = (("qx", D_ATT_X), ("kx", D_ATT_X), ("va", D_ATT), ("fa", LANES), ("cin", 2 * D_CONV),
            ("qh", D_HGRN), ("fh", D_HGRN), ("ih", D_HGRN), ("gh", D_HGRN))
_IN_WIDTHS = dict(_IN_COLS)
_IN_OFFSETS = dict(zip(_IN_WIDTHS, np.cumsum([0] + [w for _, w in _IN_COLS[:-1]]).tolist()))
_IN_OUTPUTS = (("qx", BF16), ("kx", BF16), ("va", BF16), ("cin", F32),
               ("qh", F32), ("fh", F32), ("ih", BF16), ("gh", F32))
_N_SPLIT = 3


def _nt_dot(a, b):
    return lax.dot_general(a, b, (((1,), (1,)), ((), ())), preferred_element_type=F32)


def _dot(a, b):
    return jnp.dot(a, b, preferred_element_type=F32)


def _rmsnorm_rows(x, g):
    return x * lax.rsqrt(jnp.mean(x * x, axis=-1, keepdims=True) + RMS_EPS) * g


def _compiler_params(semantics):
    return pltpu.CompilerParams(dimension_semantics=semantics, vmem_limit_bytes=VMEM_LIMIT_BYTES)


def _resident(shape):
    nd = len(shape)
    return pl.BlockSpec(shape, lambda *_: (0,) * nd, pipeline_mode=pl.Buffered(1))


def _split_bf16(x, n):
    terms = []
    for _ in range(n - 1):
        t = x.astype(BF16)
        terms.append(t)
        x = x - t.astype(F32)
    terms.append(x.astype(BF16))
    return terms


def _bias_lane_constants():
    sel = np.zeros((_N_SPLIT * LANES, 2 * D_ATT_X), np.float32)
    one = np.zeros((1, 2 * D_ATT_X), np.float32)
    for h in range(ATT_HEADS):
        base = h * LANES + HEAD_DIM
        for i in range(_N_SPLIT):
            sel[i * LANES + h, base + i] = 1.0
            one[0, base + _N_SPLIT + i] = 1.0
            one[0, D_ATT_X + base + i] = 1.0
            sel[i * LANES + h, D_ATT_X + base + _N_SPLIT + i] = -1.0
    return sel, one


def _inproj_kernel(x_ref, g_ref, w_ref, fb_ref, tri_ref, sel_ref, one_ref,
                   qx_ref, kx_ref, va_ref, cin_ref, qh_ref, fh_ref, ih_ref, gh_ref, carry_ref):
    @pl.when(pl.program_id(1) == 0)
    def _():
        carry_ref[...] = jnp.zeros_like(carry_ref)

    h = _rmsnorm_rows(x_ref[0], g_ref[...]).astype(BF16)

    def proj(name):
        off = _IN_OFFSETS[name]
        return _dot(h, w_ref[:, off:off + _IN_WIDTHS[name]])

    log_f = jax.nn.log_sigmoid(proj("fa") + fb_ref[...])
    tri = tri_ref[...]
    c = carry_ref[...] + sum(_dot(tri, term) for term in _split_bf16(log_f, _N_SPLIT))
    rows = c.shape[0]
    carry_ref[...] = c[rows - 1:rows, :]
    bias = _dot(jnp.concatenate(_split_bf16(c * LOG2E, _N_SPLIT), axis=1), sel_ref[...]) + one_ref[...]

    qx_ref[0] = (proj("qx") * (HEAD_DIM ** -0.5 * LOG2E) + bias[:, :D_ATT_X]).astype(BF16)
    kx_ref[0] = (proj("kx") + bias[:, D_ATT_X:]).astype(BF16)
    for name, o_ref in (("va", va_ref), ("cin", cin_ref), ("qh", qh_ref), ("fh", fh_ref),
                        ("ih", ih_ref), ("gh", gh_ref)):
        o_ref[0] = proj(name).astype(o_ref.dtype)


def _inproj(x, g, w, fb, tri, sel, one):
    b, t, d = x.shape
    tm = tri.shape[0]
    tile = lambda n: pl.BlockSpec((1, tm, n), lambda i, j: (i, j, 0))
    return pl.pallas_call(
        _inproj_kernel,
        grid=(b, t // tm),
        in_specs=[tile(d), _resident(g.shape), _resident(w.shape), _resident(fb.shape),
                  _resident(tri.shape), _resident(sel.shape), _resident(one.shape)],
        out_specs=[tile(_IN_WIDTHS[name]) for name, _ in _IN_OUTPUTS],
        out_shape=[jax.ShapeDtypeStruct((b, t, _IN_WIDTHS[name]), dt) for name, dt in _IN_OUTPUTS],
        scratch_shapes=[pltpu.VMEM((1, LANES), F32)],
        compiler_params=_compiler_params(("parallel", "arbitrary")),
        name="inproj",
    )(x, g, w, fb, tri, sel, one)


_FIRST, _LAST, _MASKED = 1, 2, 4


def _causal_tables(t, tq, tk):
    ii, jj, fl = [], [], []
    for i in range(t // tq):
        nj = -(-((i + 1) * tq) // tk)
        for j in range(nj):
            flag = (_FIRST if j == 0 else 0) | (_LAST if j == nj - 1 else 0)
            if (j + 1) * tk - 1 > i * tq:
                flag |= _MASKED
            ii.append(i), jj.append(j), fl.append(flag)
    return (np.asarray(ii, np.int32), np.asarray(jj, np.int32), np.asarray(fl, np.int32))


def _attn_kernel(ii_ref, jj_ref, fl_ref, q_ref, k_ref, v_ref, o_ref, m_ref, acc_ref, *, tq, tk):
    step = pl.program_id(1)
    i, j, flag = ii_ref[step], jj_ref[step], fl_ref[step]
    lane = lax.broadcasted_iota(jnp.int32, (1, LANES), 1)

    @pl.when((flag & _FIRST) != 0)
    def _():
        m_ref[...] = jnp.full_like(m_ref, -jnp.inf)
        acc_ref[...] = jnp.zeros_like(acc_ref)

    def scores(h):
        lanes = slice(h * LANES, (h + 1) * LANES)
        return _nt_dot(q_ref[0, :, lanes], k_ref[0, :, lanes])

    def block(masked):
        if masked:
            row = i * tq + lax.broadcasted_iota(jnp.int32, (tq, tk), 0)
            col = j * tk + lax.broadcasted_iota(jnp.int32, (tq, tk), 1)
            keep = col <= row
        s_next = scores(0)
        for h in range(ATT_HEADS):
            s = s_next
            if h + 1 < ATT_HEADS:
                s_next = scores(h + 1)
            if masked:
                s = jnp.where(keep, s, -jnp.inf)
            pair = slice((h // 2) * LANES, (h // 2 + 1) * LANES)
            own = (lane < HEAD_DIM) if h % 2 == 0 else (lane >= HEAD_DIM)
            v2 = v_ref[0, :, pair]
            v1 = jnp.where(own, v2, jnp.ones_like(v2))
            m_prev = m_ref[h]
            m_new = jnp.maximum(m_prev, jnp.max(s, axis=-1, keepdims=True))
            p = jnp.exp2(s - pltpu.repeat(m_new, tk // LANES, axis=1))
            acc_ref[h] = jnp.exp2(m_prev - m_new) * acc_ref[h] + _dot(p.astype(BF16), v1)
            m_ref[h] = m_new

    @pl.when((flag & _MASKED) != 0)
    def _():
        block(True)

    @pl.when((flag & _MASKED) == 0)
    def _():
        block(False)

    @pl.when((flag & _LAST) != 0)
    def _():
        for p in range(HEAD_PAIRS):
            a0, a1 = acc_ref[2 * p], acc_ref[2 * p + 1]
            o0 = a0 / pltpu.roll(a0, HEAD_DIM, axis=1)
            o1 = a1 / pltpu.roll(a1, HEAD_DIM, axis=1)
            o_ref[0, :, p * LANES:(p + 1) * LANES] = jnp.where(lane < HEAD_DIM, o0, o1).astype(o_ref.dtype)


def _attention(qx, kx, va, tq, tk):
    b, t, _ = va.shape
    ii, jj, fl = _causal_tables(t, tq, tk)
    q_map = lambda bb, s, ii, jj, fl: (bb, ii[s], 0)
    k_map = lambda bb, s, ii, jj, fl: (bb, jj[s], 0)
    grid_spec = pltpu.PrefetchScalarGridSpec(
        num_scalar_prefetch=3,
        grid=(b, len(ii)),
        in_specs=[pl.BlockSpec((1, tq, D_ATT_X), q_map),
                  pl.BlockSpec((1, tk, D_ATT_X), k_map),
                  pl.BlockSpec((1, tk, D_ATT), k_map)],
        out_specs=pl.BlockSpec((1, tq, D_ATT), q_map),
        scratch_shapes=[pltpu.VMEM((ATT_HEADS, tq, LANES), F32), pltpu.VMEM((ATT_HEADS, tq, LANES), F32)],
    )
    return pl.pallas_call(
        functools.partial(_attn_kernel, tq=tq, tk=tk),
        grid_spec=grid_spec,
        out_shape=jax.ShapeDtypeStruct((b, t, D_ATT), BF16),
        compiler_params=_compiler_params(("parallel", "arbitrary")),
        name="fox_attention",
    )(jnp.asarray(ii), jnp.asarray(jj), jnp.asarray(fl), qx, kx, va)


_CONV_ROWS = 64


def _conv_kernel(cur_ref, halo_ref, w_ref, b_ref, lg_ref, lb_ref, o_ref, hp_ref, *, tc):
    def glu(u):
        return u[:, :D_CONV] * jax.nn.sigmoid(u[:, D_CONV:])

    hp_ref[CONV_HALO:, :] = glu(cur_ref[0])
    halo = glu(halo_ref[0])
    hp_ref[0:CONV_HALO, :] = jnp.where(pl.program_id(1) > 0, halo, jnp.zeros_like(halo))
    first = CONV_HALO - (CONV_WIDTH - 1)
    for r0 in range(0, tc, _CONV_ROWS):
        acc = jnp.zeros((_CONV_ROWS, D_CONV), F32)
        for tap in range(CONV_WIDTH):
            acc = acc + hp_ref[r0 + first + tap:r0 + first + tap + _CONV_ROWS, :] * w_ref[tap:tap + 1, :]
        acc = acc + b_ref[...]
        mu = jnp.mean(acc, axis=-1, keepdims=True)
        xc = acc - mu
        var = jnp.mean(xc * xc, axis=-1, keepdims=True)
        y = xc * lax.rsqrt(var + LN_EPS) * lg_ref[...] + lb_ref[...]
        o_ref[0, r0:r0 + _CONV_ROWS, :] = (y * jax.nn.sigmoid(y)).astype(o_ref.dtype)


def _conv_module(cin, w, bias, ln_g, ln_b, tc):
    b, t, d = cin.shape
    halo_blocks = tc // CONV_HALO
    return pl.pallas_call(
        functools.partial(_conv_kernel, tc=tc),
        grid=(b, t // tc),
        in_specs=[pl.BlockSpec((1, tc, d), lambda i, j: (i, j, 0)),
                  pl.BlockSpec((1, CONV_HALO, d), lambda i, j: (i, jnp.maximum(j * halo_blocks - 1, 0), 0)),
                  _resident(w.shape), _resident(bias.shape), _resident(ln_g.shape), _resident(ln_b.shape)],
        out_specs=pl.BlockSpec((1, tc, D_CONV), lambda i, j: (i, j, 0)),
        out_shape=jax.ShapeDtypeStruct((b, t, D_CONV), BF16),
        scratch_shapes=[pltpu.VMEM((tc + CONV_HALO, D_CONV), F32)],
        compiler_params=_compiler_params(("parallel", "parallel")),
        name="conformer_conv",
    )(cin, cin, w, bias, ln_g, ln_b)


HGRN_CHUNK = 128
_HGRN_LEVELS = HGRN_CHUNK.bit_length()


def _hgrn_constants():
    c = HGRN_CHUNK
    t = np.arange(c)[:, None]
    s = np.arange(c)[None, :]
    mats = [(s <= t), (s > t)]
    for lvl in range(2, _HGRN_LEVELS):
        same = (s >> (lvl - 1)) == (t >> (lvl - 1))
        mats += [(s <= t) & same, (s > t) & same]
    stack = np.concatenate([m.astype(np.float32) for m in mats], axis=0)
    x = t ^ s
    level = np.where(t == s, 0, np.where(s < t, np.floor(np.log2(np.maximum(x, 1))).astype(np.int32) + 1, -1))
    return stack, level.astype(np.int32)


def _hgrn_kernel(q_ref, f_ref, v_ref, g_ref, lbl_ref, gn_ref, cm_ref, lvl_ref, o_ref, st_ref,
                 *, layer, chunks):
    c = HGRN_CHUNK

    @pl.when(pl.program_id(2) == 0)
    def _():
        st_ref[...] = jnp.zeros_like(st_ref)

    z = lbl_ref[...]
    e = jnp.exp(z - jnp.max(z, axis=0, keepdims=True))
    if layer > 0:
        lb = jnp.sum(e[1:layer + 1], axis=0, keepdims=True) / jnp.sum(e, axis=0, keepdims=True)
    else:
        lb = jnp.zeros((1, LANES), F32)

    lane = lax.broadcasted_iota(jnp.int32, (1, LANES), 1)
    head0 = lane < HEAD_DIM
    level = lvl_ref[...]
    vi = lax.broadcasted_iota(jnp.int32, (LANES, LANES), 0) // HEAD_DIM
    di = lax.broadcasted_iota(jnp.int32, (LANES, LANES), 1) // HEAD_DIM
    same_head = vi == di

    for ch in range(chunks):
        rows = slice(ch * c, (ch + 1) * c)
        q = q_ref[0, rows, :]
        v2 = v_ref[0, rows, :]
        f = lb + (1.0 - lb) * jax.nn.sigmoid(f_ref[0, rows, :])
        kk = 1.0 - f
        g = jnp.log(f)
        g1 = g.astype(BF16)
        g2 = (g - g1.astype(F32)).astype(BF16)
        sums = _dot(cm_ref[...], jnp.concatenate([g1, g2], axis=1))
        sums = sums[:, :LANES] + sums[:, LANES:]

        def seg(idx):
            return sums[idx * c:(idx + 1) * c, :]

        b_incl, b_rest = seg(0), seg(1)
        k_b = kk.astype(BF16)
        a = [jnp.zeros((c, c), F32), jnp.zeros((c, c), F32)]
        for lvl in range(_HGRN_LEVELS):
            if lvl == 0:
                qt, kt = q, k_b
            elif lvl == 1:
                qt, kt = q * jnp.exp(g), k_b
            else:
                qt = q * jnp.exp(seg(2 * (lvl - 1)))
                kt = (kk * jnp.exp(seg(2 * (lvl - 1) + 1))).astype(BF16)
            for hh in range(2):
                qm = jnp.where(head0 if hh == 0 else ~head0, qt, 0.0).astype(BF16)
                a[hh] = jnp.where(level == lvl, _nt_dot(qm, kt), a[hh])
        o = jnp.where(head0, _dot(a[0].astype(BF16), v2), _dot(a[1].astype(BF16), v2))
        st = st_ref[...]
        o = o + _nt_dot((q * jnp.exp(b_incl)).astype(BF16), st.astype(BF16))
        v_t = v2.astype(F32).T.astype(BF16)
        upd = _dot(v_t, (kk * jnp.exp(b_rest)).astype(BF16))
        st_ref[...] = st * jnp.exp(b_incl[c - 1:c, :]) + jnp.where(same_head, upd, 0.0)

        o2 = o * o
        ms0 = jnp.sum(jnp.where(head0, o2, 0.0), axis=-1, keepdims=True) * (1.0 / HEAD_DIM)
        ms1 = jnp.sum(jnp.where(head0, 0.0, o2), axis=-1, keepdims=True) * (1.0 / HEAD_DIM)
        inv = jnp.where(head0, lax.rsqrt(ms0 + RMS_EPS), lax.rsqrt(ms1 + RMS_EPS))
        gate = g_ref[0, rows, :]
        o_ref[0, rows, :] = (o * inv * gn_ref[...] * (gate * jax.nn.sigmoid(gate))).astype(o_ref.dtype)


def _hgrn(qh, fh, ih, gh, lb_logits, gn, cmat, lvl, layer, tt):
    b, t, _ = qh.shape
    tile = pl.BlockSpec((1, tt, LANES), lambda i, p, j: (i, j, p))
    return pl.pallas_call(
        functools.partial(_hgrn_kernel, layer=layer, chunks=tt // HGRN_CHUNK),
        grid=(b, HEAD_PAIRS, t // tt),
        in_specs=[tile, tile, tile, tile,
                  pl.BlockSpec((lb_logits.shape[0], LANES), lambda i, p, j: (0, p)),
                  _resident(gn.shape), _resident(cmat.shape), _resident(lvl.shape)],
        out_specs=tile,
        out_shape=jax.ShapeDtypeStruct((b, t, D_HGRN), BF16),
        scratch_shapes=[pltpu.VMEM((LANES, LANES), F32)],
        compiler_params=_compiler_params(("parallel", "parallel", "arbitrary")),
        name="hgrn2",
    )(qh, fh, ih, gh, lb_logits, gn, cmat, lvl)


_FFN_COLS = 256


def _ffn_kernel(x_ref, att_ref, cnv_ref, hg_ref, wo_ref, g_ref, wi_ref, w2_ref, gf_ref, o_ref, a_ref,
                *, d_ff, final):
    mix = jnp.concatenate([att_ref[...], cnv_ref[...], hg_ref[...]], axis=1)
    x1 = x_ref[...] + _dot(mix, wo_ref[...])
    h = _rmsnorm_rows(x1, g_ref[...]).astype(BF16)
    for c0 in range(0, d_ff, _FFN_COLS):
        gate = _dot(h, wi_ref[:, c0:c0 + _FFN_COLS])
        up = _dot(h, wi_ref[:, d_ff + c0:d_ff + c0 + _FFN_COLS])
        a_ref[:, c0:c0 + _FFN_COLS] = (gate * jax.nn.sigmoid(gate) * up).astype(BF16)
    y = x1 + _dot(a_ref[...], w2_ref[...])
    if final:
        y = _rmsnorm_rows(y, gf_ref[...])
    o_ref[...] = y


def _out_ffn(x2d, att, cnv, hg, wo, g, wi, w2, gf, tm, final):
    m, d = x2d.shape
    d_ff = w2.shape[0]
    row = lambda n: pl.BlockSpec((tm, n), lambda i: (i, 0))
    return pl.pallas_call(
        functools.partial(_ffn_kernel, d_ff=d_ff, final=final),
        grid=(m // tm,),
        in_specs=[row(d), row(D_ATT), row(D_CONV), row(D_HGRN), _resident(wo.shape), _resident(g.shape),
                  _resident(wi.shape), _resident(w2.shape), _resident(gf.shape)],
        out_specs=row(d),
        out_shape=jax.ShapeDtypeStruct((m, d), F32),
        scratch_shapes=[pltpu.VMEM((tm, d_ff), BF16)],
        compiler_params=_compiler_params(("parallel",)),
        name="out_ffn",
    )(x2d, att, cnv, hg, wo, g, wi, w2, gf)


def _tiles(t):
    return dict(tm=min(512, t), tq=min(512, t), tk=min(512, t), tc=min(512, t), tt=min(256, t))


def _rearranged_w_in(w_in_l):
    sizes = (D_ATT, D_ATT, D_ATT, ATT_HEADS, 2 * D_CONV, D_HGRN, D_HGRN, D_HGRN, D_HGRN)
    pts = np.cumsum(sizes)[:-1].tolist()
    q, k, v, f, cin, qh, fh, ih, gh = jnp.split(w_in_l, pts, axis=-1)
    d = w_in_l.shape[0]

    def per_head_tile(w):
        w = w.reshape(d, ATT_HEADS, HEAD_DIM)
        return jnp.pad(w, ((0, 0), (0, 0), (0, LANES - HEAD_DIM))).reshape(d, D_ATT_X)

    f = jnp.pad(f, ((0, 0), (0, LANES - ATT_HEADS)))
    return jnp.concatenate([per_head_tile(q), per_head_tile(k), v, f, cin, qh, fh, ih, gh], axis=-1).astype(BF16)


def kernel(x, norm_mix_g, w_in, fgate_b, conv_w, conv_b, conv_ln_g, conv_ln_b, hgrn_lb_logits, hgrn_norm_g,
           w_out, norm_ffn_g, w_ffn_in, w_ffn_out, norm_final_g):
    b, t, d = x.shape
    depth = w_in.shape[0]
    tl = _tiles(t)
    row = lambda a: a.reshape(1, -1).astype(F32)

    tri = jnp.asarray(np.tril(np.ones((tl["tm"], tl["tm"]), np.float32)), BF16)
    sel, one = _bias_lane_constants()
    sel, one = jnp.asarray(sel, BF16), jnp.asarray(one)
    cstack, lvl = _hgrn_constants()
    cmat = jnp.asarray(cstack, BF16)
    lvl = jnp.asarray(lvl)
    lb_logits = hgrn_lb_logits.astype(F32)
    gf = row(norm_final_g)

    for l in range(depth):
        w = _rearranged_w_in(w_in[l])
        fb = row(jnp.pad(fgate_b[l], (0, LANES - ATT_HEADS)))
        qx, kx, va, cin, qh, fh, ih, gh = _inproj(x, row(norm_mix_g[l]), w, fb, tri, sel, one)

        att = _attention(qx, kx, va, tl["tq"], tl["tk"])

        cw = jnp.concatenate([conv_w[l], jnp.zeros((1, D_CONV), F32)], axis=0)
        cnv = _conv_module(cin, cw, row(conv_b[l]), row(conv_ln_g[l]), row(conv_ln_b[l]), tl["tc"])

        gn = row(jnp.tile(hgrn_norm_g[l], LANES // HEAD_DIM))
        hg = _hgrn(qh, fh, ih, gh, lb_logits, gn, cmat, lvl, l, tl["tt"])

        x = _out_ffn(x.reshape(b * t, d), att.reshape(b * t, -1), cnv.reshape(b * t, -1), hg.reshape(b * t, -1),
                     w_out[l].astype(BF16), row(norm_ffn_g[l]), w_ffn_in[l].astype(BF16),
                     w_ffn_out[l].astype(BF16), gf, tl["tm"], final=(l == depth - 1)).reshape(b, t, d)
    return x
```

```python
import functools

import numpy as np
import jax
import jax.numpy as jnp
from jax import lax
from jax.experimental import pallas as pl
from jax.experimental.pallas import tpu as pltpu

F32 = jnp.float32
BF16 = jnp.bfloat16

HEAD_DIM = 64
ATT_HEADS = 6
D_ATT = ATT_HEADS * HEAD_DIM
D_CONV = 256
CONV_WIDTH = 31
HGRN_HEADS = 6
D_HGRN = HGRN_HEADS * HEAD_DIM
RMS_EPS = 1e-6
LN_EPS = 1e-5

LANES = 128
HEAD_PAIRS = D_ATT // LANES
CONV_HALO = 32
VMEM_LIMIT_BYTES = 56 * 1024 * 1024

D_ATT_X = ATT_HEADS * LANES
LOG2E = 1.4426950408889634

_IN_COLS = (("qx", D_ATT_X), ("kx", D_ATT_X), ("va", D_ATT), ("fa", LANES), ("cin", 2 * D_CONV),
            ("qh", D_HGRN), ("fh", D_HGRN), ("ih", D_HGRN), ("gh", D_HGRN))
_IN_WIDTHS = dict(_IN_COLS)
_IN_OFFSETS = dict(zip(_IN_WIDTHS, np.cumsum([0] + [w for _, w in _IN_COLS[:-1]]).tolist()))
_IN_OUTPUTS = (("qx", BF16), ("kx", BF16), ("va", BF16), ("cin", F32),
               ("qh", F32), ("fh", F32), ("ih", BF16), ("gh", F32))
_N_SPLIT = 3


def _nt_dot(a, b):
    return lax.dot_general(a, b, (((1,), (1,)), ((), ())), preferred_element_type=F32)


def _dot(a, b):
    return jnp.dot(a, b, preferred_element_type=F32)


def _rmsnorm_rows(x, g):
    return x * lax.rsqrt(jnp.mean(x * x, axis=-1, keepdims=True) + RMS_EPS) * g


def _compiler_params(semantics):
    return pltpu.CompilerParams(dimension_semantics=semantics, vmem_limit_bytes=VMEM_LIMIT_BYTES)


def _resident(shape):
    nd = len(shape)
    return pl.BlockSpec(shape, lambda *_: (0,) * nd, pipeline_mode=pl.Buffered(1))


def _split_bf16(x, n):
    terms = []
    for _ in range(n - 1):
        t = x.astype(BF16)
        terms.append(t)
        x = x - t.astype(F32)
    terms.append(x.astype(BF16))
    return terms


def _bias_lane_constants():
    sel = np.zeros((_N_SPLIT * LANES, 2 * D_ATT_X), np.float32)
    one = np.zeros((1, 2 * D_ATT_X), np.float32)
    for h in range(ATT_HEADS):
        base = h * LANES + HEAD_DIM
        for i in range(_N_SPLIT):
            sel[i * LANES + h, base + i] = 1.0
            one[0, base + _N_SPLIT + i] = 1.0
            one[0, D_ATT_X + base + i] = 1.0
            sel[i * LANES + h, D_ATT_X + base + _N_SPLIT + i] = -1.0
    return sel, one


def _inproj_kernel(x_ref, g_ref, w_ref, fb_ref, tri_ref, sel_ref, one_ref,
                   qx_ref, kx_ref, va_ref, cin_ref, qh_ref, fh_ref, ih_ref, gh_ref, carry_ref):
    @pl.when(pl.program_id(1) == 0)
    def _():
        carry_ref[...] = jnp.zeros_like(carry_ref)

    h = _rmsnorm_rows(x_ref[0], g_ref[...]).astype(BF16)

    def proj(name):
        off = _IN_OFFSETS[name]
        return _dot(h, w_ref[:, off:off + _IN_WIDTHS[name]])

    log_f = jax.nn.log_sigmoid(proj("fa") + fb_ref[...])
    tri = tri_ref[...]
    c = carry_ref[...] + sum(_dot(tri, term) for term in _split_bf16(log_f, _N_SPLIT))
    rows = c.shape[0]
    carry_ref[...] = c[rows - 1:rows, :]
    bias = _dot(jnp.concatenate(_split_bf16(c * LOG2E, _N_SPLIT), axis=1), sel_ref[...]) + one_ref[...]

    qx_ref[0] = (proj("qx") * (HEAD_DIM ** -0.5 * LOG2E) + bias[:, :D_ATT_X]).astype(BF16)
    kx_ref[0] = (proj("kx") + bias[:, D_ATT_X:]).astype(BF16)
    for name, o_ref in (("va", va_ref), ("cin", cin_ref), ("qh", qh_ref), ("fh", fh_ref),
                        ("ih", ih_ref), ("gh", gh_ref)):
        o_ref[0] = proj(name).astype(o_ref.dtype)


def _inproj(x, g, w, fb, tri, sel, one):
    b, t, d = x.shape
    tm = tri.shape[0]
    tile = lambda n: pl.BlockSpec((1, tm, n), lambda i, j: (i, j, 0))
    return pl.pallas_call(
        _inproj_kernel,
        grid=(b, t // tm),
        in_specs=[tile(d), _resident(g.shape), _resident(w.shape), _resident(fb.shape),
                  _resident(tri.shape), _resident(sel.shape), _resident(one.shape)],
        out_specs=[tile(_IN_WIDTHS[name]) for name, _ in _IN_OUTPUTS],
        out_shape=[jax.ShapeDtypeStruct((b, t, _IN_WIDTHS[name]), dt) for name, dt in _IN_OUTPUTS],
        scratch_shapes=[pltpu.VMEM((1, LANES), F32)],
        compiler_params=_compiler_params(("parallel", "arbitrary")),
        name="inproj",
    )(x, g, w, fb, tri, sel, one)


_FIRST, _LAST, _MASKED = 1, 2, 4


def _causal_tables(t, tq, tk):
    ii, jj, fl = [], [], []
    for i in range(t // tq):
        nj = -(-((i + 1) * tq) // tk)
        for j in range(nj):
            flag = (_FIRST if j == 0 else 0) | (_LAST if j == nj - 1 else 0)
            if (j + 1) * tk - 1 > i * tq:
                flag |= _MASKED
            ii.append(i), jj.append(j), fl.append(flag)
    return (np.asarray(ii, np.int32), np.asarray(jj, np.int32), np.asarray(fl, np.int32))


def _attn_kernel(ii_ref, jj_ref, fl_ref, q_ref, k_ref, v_ref, o_ref, m_ref, acc_ref, *, tq, tk):
    step = pl.program_id(1)
    i, j, flag = ii_ref[step], jj_ref[step], fl_ref[step]
    lane = lax.broadcasted_iota(jnp.int32, (1, LANES), 1)

    @pl.when((flag & _FIRST) != 0)
    def _():
        m_ref[...] = jnp.full_like(m_ref, -jnp.inf)
        acc_ref[...] = jnp.zeros_like(acc_ref)

    def scores(h):
        lanes = slice(h * LANES, (h + 1) * LANES)
        return _nt_dot(q_ref[0, :, lanes], k_ref[0, :, lanes])

    def block(masked):
        if masked:
            row = i * tq + lax.broadcasted_iota(jnp.int32, (tq, tk), 0)
            col = j * tk + lax.broadcasted_iota(jnp.int32, (tq, tk), 1)
            keep = col <= row
        s_next = scores(0)
        for h in range(ATT_HEADS):
            s = s_next
            if h + 1 < ATT_HEADS:
                s_next = scores(h + 1)
            if masked:
                s = jnp.where(keep, s, -jnp.inf)
            pair = slice((h // 2) * LANES, (h // 2 + 1) * LANES)
            own = (lane < HEAD_DIM) if h % 2 == 0 else (lane >= HEAD_DIM)
            v2 = v_ref[0, :, pair]
            v1 = jnp.where(own, v2, jnp.ones_like(v2))
            m_prev = m_ref[h]
            m_new = jnp.maximum(m_prev, jnp.max(s, axis=-1, keepdims=True))
            p = jnp.exp2(s - pltpu.repeat(m_new, tk // LANES, axis=1))
            acc_ref[h] = jnp.exp2(m_prev - m_new) * acc_ref[h] + _dot(p.astype(BF16), v1)
            m_ref[h] = m_new

    @pl.when((flag & _MASKED) != 0)
    def _():
        block(True)

    @pl.when((flag & _MASKED) == 0)
    def _():
        block(False)

    @pl.when((flag & _LAST) != 0)
    def _():
        for p in range(HEAD_PAIRS):
            a0, a1 = acc_ref[2 * p], acc_ref[2 * p + 1]
            o0 = a0 / pltpu.roll(a0, HEAD_DIM, axis=1)
            o1 = a1 / pltpu.roll(a1, HEAD_DIM, axis=1)
            o_ref[0, :, p * LANES:(p + 1) * LANES] = jnp.where(lane < HEAD_DIM, o0, o1).astype(o_ref.dtype)


def _attention(qx, kx, va, tq, tk):
    b, t, _ = va.shape
    ii, jj, fl = _causal_tables(t, tq, tk)
    q_map = lambda bb, s, ii, jj, fl: (bb, ii[s], 0)
    k_map = lambda bb, s, ii, jj, fl: (bb, jj[s], 0)
    grid_spec = pltpu.PrefetchScalarGridSpec(
        num_scalar_prefetch=3,
        grid=(b, len(ii)),
        in_specs=[pl.BlockSpec((1, tq, D_ATT_X), q_map),
                  pl.BlockSpec((1, tk, D_ATT_X), k_map),
                  pl.BlockSpec((1, tk, D_ATT), k_map)],
        out_specs=pl.BlockSpec((1, tq, D_ATT), q_map),
        scratch_shapes=[pltpu.VMEM((ATT_HEADS, tq, LANES), F32), pltpu.VMEM((ATT_HEADS, tq, LANES), F32)],
    )
    return pl.pallas_call(
        functools.partial(_attn_kernel, tq=tq, tk=tk),
        grid_spec=grid_spec,
        out_shape=jax.ShapeDtypeStruct((b, t, D_ATT), BF16),
        compiler_params=_compiler_params(("parallel", "arbitrary")),
        name="fox_attention",
    )(jnp.asarray(ii), jnp.asarray(jj), jnp.asarray(fl), qx, kx, va)


_CONV_ROWS = 64


def _conv_kernel(cur_ref, halo_ref, w_ref, b_ref, lg_ref, lb_ref, o_ref, hp_ref, *, tc):
    def glu(u):
        return u[:, :D_CONV] * jax.nn.sigmoid(u[:, D_CONV:])

    hp_ref[CONV_HALO:, :] = glu(cur_ref[0])
    halo = glu(halo_ref[0])
    hp_ref[0:CONV_HALO, :] = jnp.where(pl.program_id(1) > 0, halo, jnp.zeros_like(halo))
    first = CONV_HALO - (CONV_WIDTH - 1)
    for r0 in range(0, tc, _CONV_ROWS):
        acc = jnp.zeros((_CONV_ROWS, D_CONV), F32)
        for tap in range(CONV_WIDTH):
            acc = acc + hp_ref[r0 + first + tap:r0 + first + tap + _CONV_ROWS, :] * w_ref[tap:tap + 1, :]
        acc = acc + b_ref[...]
        mu = jnp.mean(acc, axis=-1, keepdims=True)
        xc = acc - mu
        var = jnp.mean(xc * xc, axis=-1, keepdims=True)
        y = xc * lax.rsqrt(var + LN_EPS) * lg_ref[...] + lb_ref[...]
        o_ref[0, r0:r0 + _CONV_ROWS, :] = (y * jax.nn.sigmoid(y)).astype(o_ref.dtype)


def _conv_module(cin, w, bias, ln_g, ln_b, tc):
    b, t, d = cin.shape
    halo_blocks = tc // CONV_HALO
    return pl.pallas_call(
        functools.partial(_conv_kernel, tc=tc),
        grid=(b, t // tc),
        in_specs=[pl.BlockSpec((1, tc, d), lambda i, j: (i, j, 0)),
                  pl.BlockSpec((1, CONV_HALO, d), lambda i, j: (i, jnp.maximum(j * halo_blocks - 1, 0), 0)),
                  _resident(w.shape), _resident(bias.shape), _resident(ln_g.shape), _resident(ln_b.shape)],
        out_specs=pl.BlockSpec((1, tc, D_CONV), lambda i, j: (i, j, 0)),
        out_shape=jax.ShapeDtypeStruct((b, t, D_CONV), BF16),
        scratch_shapes=[pltpu.VMEM((tc + CONV_HALO, D_CONV), F32)],
        compiler_params=_compiler_params(("parallel", "parallel")),
        name="conformer_conv",
    )(cin, cin, w, bias, ln_g, ln_b)


HGRN_CHUNK = 128
_HGRN_LEVELS = HGRN_CHUNK.bit_length()


def _hgrn_level_map():
    t = np.arange(HGRN_CHUNK)[:, None]
    s = np.arange(HGRN_CHUNK)[None, :]
    high_bit = np.floor(np.log2(np.maximum(t ^ s, 1))).astype(np.int32)
    level = np.where(t == s, 0, np.where(s < t, high_bit + 1, -1)).astype(np.int32)
    return np.concatenate([level, level], axis=1)


def _hgrn_kernel(q_ref, f_ref, v_ref, g_ref, lbl_ref, gn_ref, tri_ref, lvl_ref, o_ref, st_ref,
                 *, layer, chunks):
    c = HGRN_CHUNK

    @pl.when(pl.program_id(2) == 0)
    def _():
        st_ref[...] = jnp.zeros_like(st_ref)

    z = lbl_ref[...]
    e = jnp.exp(z - jnp.max(z, axis=0, keepdims=True))
    if layer > 0:
        lb = jnp.sum(e[1:layer + 1], axis=0, keepdims=True) / jnp.sum(e, axis=0, keepdims=True)
    else:
        lb = jnp.zeros((1, LANES), F32)

    lane = lax.broadcasted_iota(jnp.int32, (1, LANES), 1)
    head0 = lane < HEAD_DIM
    row = lax.broadcasted_iota(jnp.int32, (c, LANES), 0)
    level = lvl_ref[...]
    vi = lax.broadcasted_iota(jnp.int32, (LANES, LANES), 0) // HEAD_DIM
    di = lax.broadcasted_iota(jnp.int32, (LANES, LANES), 1) // HEAD_DIM
    same_head = vi == di

    def head_stack(x):
        return jnp.concatenate([jnp.where(head0, x, 0.0), jnp.where(head0, 0.0, x)], axis=0).astype(BF16)

    for ch in range(chunks):
        rows = slice(ch * c, (ch + 1) * c)
        q = q_ref[0, rows, :]
        v2 = v_ref[0, rows, :].astype(F32)
        f = lb + (1.0 - lb) * jax.nn.sigmoid(f_ref[0, rows, :])
        kk = 1.0 - f
        g = jnp.log(f)
        b2 = _dot(tri_ref[...], jnp.concatenate(_split_bf16(g, 2), axis=1))
        b_incl = b2[:, :LANES] + b2[:, LANES:]

        kk_stack = head_stack(kk)
        a = jnp.zeros((c, 2 * c), F32)
        seg_end = b_incl
        for lvl in range(_HGRN_LEVELS):
            m = 1 << max(lvl - 1, 0)
            if lvl == 0:
                qt, kt = q, kk_stack
            elif lvl == 1:
                qt, kt = q * jnp.exp(g), kk_stack
            else:
                half = m // 2
                seg_end = jnp.where((row & half) == 0, pltpu.roll(seg_end, c - half, axis=0), seg_end)
                prev_end = jnp.where(row >= m, pltpu.roll(seg_end, m, axis=0), 0.0)
                qt = q * jnp.exp(b_incl - prev_end)
                kt = head_stack(kk * jnp.exp(seg_end - b_incl))
            a = jnp.where(level == lvl, _nt_dot(qt.astype(BF16), kt), a)
        o = _dot(a.astype(BF16), head_stack(v2))
        st = st_ref[...]
        o = o + _nt_dot((q * jnp.exp(b_incl)).astype(BF16), st.astype(BF16))
        b_end = b_incl[c - 1:c, :]
        upd = _dot(v2.T.astype(BF16), (kk * jnp.exp(b_end - b_incl)).astype(BF16))
        st_ref[...] = st * jnp.exp(b_end) + jnp.where(same_head, upd, 0.0)

        o2 = o * o
        ms0 = jnp.sum(jnp.where(head0, o2, 0.0), axis=-1, keepdims=True) * (1.0 / HEAD_DIM)
        ms1 = jnp.sum(jnp.where(head0, 0.0, o2), axis=-1, keepdims=True) * (1.0 / HEAD_DIM)
        inv = jnp.where(head0, lax.rsqrt(ms0 + RMS_EPS), lax.rsqrt(ms1 + RMS_EPS))
        gate = g_ref[0, rows, :]
        o_ref[0, rows, :] = (o * inv * gn_ref[...] * (gate * jax.nn.sigmoid(gate))).astype(o_ref.dtype)


def _hgrn(qh, fh, ih, gh, lb_logits, gn, tri, lvl, layer, tt):
    b, t, _ = qh.shape
    tile = pl.BlockSpec((1, tt, LANES), lambda i, p, j: (i, j, p))
    return pl.pallas_call(
        functools.partial(_hgrn_kernel, layer=layer, chunks=tt // HGRN_CHUNK),
        grid=(b, HEAD_PAIRS, t // tt),
        in_specs=[tile, tile, tile, tile,
                  pl.BlockSpec((lb_logits.shape[0], LANES), lambda i, p, j: (0, p)),
                  _resident(gn.shape), _resident(tri.shape), _resident(lvl.shape)],
        out_specs=tile,
        out_shape=jax.ShapeDtypeStruct((b, t, D_HGRN), BF16),
        scratch_shapes=[pltpu.VMEM((LANES, LANES), F32)],
        compiler_params=_compiler_params(("parallel", "parallel", "arbitrary")),
        name="hgrn2",
    )(qh, fh, ih, gh, lb_logits, gn, tri, lvl)


_FFN_COLS = 256


def _ffn_kernel(x_ref, att_ref, cnv_ref, hg_ref, wo_ref, g_ref, wi_ref, w2_ref, gf_ref, o_ref, a_ref,
                *, d_ff, final):
    mix = jnp.concatenate([att_ref[...], cnv_ref[...], hg_ref[...]], axis=1)
    x1 = x_ref[...] + _dot(mix, wo_ref[...])
    h = _rmsnorm_rows(x1, g_ref[...]).astype(BF16)
    for c0 in range(0, d_ff, _FFN_COLS):
        gate = _dot(h, wi_ref[:, c0:c0 + _FFN_COLS])
        up = _dot(h, wi_ref[:, d_ff + c0:d_ff + c0 + _FFN_COLS])
        a_ref[:, c0:c0 + _FFN_COLS] = (gate * jax.nn.sigmoid(gate) * up).astype(BF16)
    y = x1 + _dot(a_ref[...], w2_ref[...])
    if final:
        y = _rmsnorm_rows(y, gf_ref[...])
    o_ref[...] = y


def _out_ffn(x2d, att, cnv, hg, wo, g, wi, w2, gf, tm, final):
    m, d = x2d.shape
    d_ff = w2.shape[0]
    row = lambda n: pl.BlockSpec((tm, n), lambda i: (i, 0))
    return pl.pallas_call(
        functools.partial(_ffn_kernel, d_ff=d_ff, final=final),
        grid=(m // tm,),
        in_specs=[row(d), row(D_ATT), row(D_CONV), row(D_HGRN), _resident(wo.shape), _resident(g.shape),
                  _resident(wi.shape), _resident(w2.shape), _resident(gf.shape)],
        out_specs=row(d),
        out_shape=jax.ShapeDtypeStruct((m, d), F32),
        scratch_shapes=[pltpu.VMEM((tm, d_ff), BF16)],
        compiler_params=_compiler_params(("parallel",)),
        name="out_ffn",
    )(x2d, att, cnv, hg, wo, g, wi, w2, gf)


def _tiles(t):
    return dict(tm=min(512, t), tq=min(512, t), tk=min(512, t), tc=min(512, t), tt=min(256, t))


_W_IN_SIZES = (D_ATT, D_ATT, D_ATT, ATT_HEADS, 2 * D_CONV, D_HGRN, D_HGRN, D_HGRN, D_HGRN)
_W_IN_AFTER_F = 3 * D_ATT + ATT_HEADS
_W_IN_TAIL = sum(_W_IN_SIZES[4:])


def _w_in_layout_kernel(w_ref, o_ref):
    x = w_ref[0]
    lane = lax.broadcasted_iota(jnp.int32, (1, LANES), 1)
    zero = jnp.zeros((x.shape[0], LANES), F32)

    def per_head_tiles(col0):
        tiles = []
        for h in range(ATT_HEADS):
            src = col0 + h * HEAD_DIM
            blk = x[:, src // LANES * LANES:src // LANES * LANES + LANES]
            if src % LANES:
                blk = pltpu.roll(blk, HEAD_DIM, axis=1)
            tiles.append(jnp.where(lane < HEAD_DIM, blk, zero))
        return tiles

    f_tile = jnp.where(lane < ATT_HEADS, x[:, 3 * D_ATT:3 * D_ATT + LANES], zero)
    shift = _W_IN_AFTER_F % LANES
    tail0 = _W_IN_AFTER_F - shift
    tail = pltpu.roll(x, x.shape[1] - shift, axis=1)[:, tail0:tail0 + _W_IN_TAIL]
    pieces = per_head_tiles(0) + per_head_tiles(D_ATT) + [x[:, 2 * D_ATT:3 * D_ATT], f_tile, tail]
    o_ref[0] = jnp.concatenate(pieces, axis=1).astype(o_ref.dtype)


def _w_in_layout(w_in):
    depth, d, d_in = w_in.shape
    rows = min(256, d)
    width = -(-d_in // LANES) * LANES
    d_out = sum(n for _, n in _IN_COLS)
    return pl.pallas_call(
        _w_in_layout_kernel,
        grid=(depth, d // rows),
        in_specs=[pl.BlockSpec((1, rows, width), lambda l, i: (l, i, 0))],
        out_specs=pl.BlockSpec((1, rows, d_out), lambda l, i: (l, i, 0)),
        out_shape=jax.ShapeDtypeStruct((depth, d, d_out), BF16),
        compiler_params=_compiler_params(("parallel", "parallel")),
        name="w_in_layout",
    )(w_in)


def kernel(x, norm_mix_g, w_in, fgate_b, conv_w, conv_b, conv_ln_g, conv_ln_b, hgrn_lb_logits, hgrn_norm_g,
           w_out, norm_ffn_g, w_ffn_in, w_ffn_out, norm_final_g):
    b, t, d = x.shape
    depth = w_in.shape[0]
    tl = _tiles(t)
    row = lambda a: a.reshape(1, -1).astype(F32)

    tri = jnp.asarray(np.tril(np.ones((tl["tm"], tl["tm"]), np.float32)), BF16)
    sel, one = _bias_lane_constants()
    sel, one = jnp.asarray(sel, BF16), jnp.asarray(one)
    tri_h = jnp.asarray(np.tril(np.ones((HGRN_CHUNK, HGRN_CHUNK), np.float32)), BF16)
    lvl = jnp.asarray(_hgrn_level_map())
    lb_logits = hgrn_lb_logits.astype(F32)
    gf = row(norm_final_g)

    w_all = _w_in_layout(w_in)
    for l in range(depth):
        w = w_all[l]
        fb = row(jnp.pad(fgate_b[l], (0, LANES - ATT_HEADS)))
        qx, kx, va, cin, qh, fh, ih, gh = _inproj(x, row(norm_mix_g[l]), w, fb, tri, sel, one)

        att = _attention(qx, kx, va, tl["tq"], tl["tk"])

        cw = jnp.concatenate([conv_w[l], jnp.zeros((1, D_CONV), F32)], axis=0)
        cnv = _conv_module(cin, cw, row(conv_b[l]), row(conv_ln_g[l]), row(conv_ln_b[l]), tl["tc"])

        gn = row(jnp.tile(hgrn_norm_g[l], LANES // HEAD_DIM))
        hg = _hgrn(qh, fh, ih, gh, lb_logits, gn, tri_h, lvl, l, tl["tt"])

        x = _out_ffn(x.reshape(b * t, d), att.reshape(b * t, -1), cnv.reshape(b * t, -1), hg.reshape(b * t, -1),
                     w_out[l].astype(BF16), row(norm_ffn_g[l]), w_ffn_in[l].astype(BF16),
                     w_ffn_out[l].astype(BF16), gf, tl["tm"], final=(l == depth - 1)).reshape(b, t, d)
    return x
```

```python
import functools

import numpy as np
import jax
import jax.numpy as jnp
from jax import lax
from jax.experimental import pallas as pl
from jax.experimental.pallas import tpu as pltpu

F32 = jnp.float32
BF16 = jnp.bfloat16

HEAD_DIM = 64
ATT_HEADS = 6
D_ATT = ATT_HEADS * HEAD_DIM
D_CONV = 256
CONV_WIDTH = 31
HGRN_HEADS = 6
D_HGRN = HGRN_HEADS * HEAD_DIM
RMS_EPS = 1e-6
LN_EPS = 1e-5

LANES = 128
HEAD_PAIRS = D_ATT // LANES
CONV_HALO = 32
VMEM_LIMIT_BYTES = 56 * 1024 * 1024

D_ATT_X = ATT_HEADS * LANES
LOG2E = 1.4426950408889634

_IN_COLS = (("va", D_ATT), ("fa", LANES), ("cin", 2 * D_CONV), ("qh", D_HGRN), ("fh", D_HGRN),
            ("ih", D_HGRN), ("gh", D_HGRN), ("qa", D_ATT), ("ka", D_ATT))
_IN_TILES = tuple((name, i) for name, n in _IN_COLS for i in range(n // LANES))
MXU_COLS = 2 * LANES
_IN_CHUNK_ORDER = (1, 2, 3, 0) + tuple(range(4, len(_IN_TILES) * LANES // MXU_COLS))
_IN_OUTPUTS = (("qx", D_ATT_X, BF16), ("kx", D_ATT_X, BF16), ("va", D_ATT, BF16), ("cnv", D_CONV, BF16),
               ("qh", D_HGRN, F32), ("fh", D_HGRN, F32), ("ih", D_HGRN, BF16), ("gh", D_HGRN, F32))
_N_SPLIT = 3


def _nt_dot(a, b):
    return lax.dot_general(a, b, (((1,), (1,)), ((), ())), preferred_element_type=F32)


def _dot(a, b):
    return jnp.dot(a, b, preferred_element_type=F32)


def _rmsnorm_rows(x, g):
    return x * lax.rsqrt(jnp.mean(x * x, axis=-1, keepdims=True) + RMS_EPS) * g


def _compiler_params(semantics):
    return pltpu.CompilerParams(dimension_semantics=semantics, vmem_limit_bytes=VMEM_LIMIT_BYTES)


def _resident(shape):
    nd = len(shape)
    return pl.BlockSpec(shape, lambda *_: (0,) * nd, pipeline_mode=pl.Buffered(1))


def _split_bf16(x, n):
    terms = []
    for _ in range(n - 1):
        t = x.astype(BF16)
        terms.append(t)
        x = x - t.astype(F32)
    terms.append(x.astype(BF16))
    return terms


_BIAS_LANES = 2 * _N_SPLIT
_BIAS_STRIDE = 8


def _bias_lane_constants():
    sel = np.zeros((_N_SPLIT * LANES, 2 * LANES), np.float32)
    one = np.zeros((1, 2 * LANES), np.float32)
    for h in range(ATT_HEADS):
        base = h * _BIAS_STRIDE
        for i in range(_N_SPLIT):
            sel[i * LANES + h, base + i] = 1.0
            one[0, base + _N_SPLIT + i] = 1.0
            one[0, LANES + base + i] = 1.0
            sel[i * LANES + h, LANES + base + _N_SPLIT + i] = -1.0
    return sel, one


_CONV_ROWS = 32
_SUBLANES = 8


def _inproj_kernel(x_ref, g_ref, w_ref, fb_ref, tri_ref, sel_ref, one_ref, cw_ref, cb_ref, lg_ref, lb_ref,
                   qx_ref, kx_ref, va_ref, cnv_ref, qh_ref, fh_ref, ih_ref, gh_ref,
                   carry_ref, halo_ref, hs_ref):
    tm = x_ref.shape[1]

    @pl.when(pl.program_id(1) == 0)
    def _():
        carry_ref[...] = jnp.zeros_like(carry_ref)
        halo_ref[...] = jnp.zeros_like(halo_ref)

    h = _rmsnorm_rows(x_ref[0], g_ref[...]).astype(BF16)
    lane = lax.broadcasted_iota(jnp.int32, (1, LANES), 1)
    direct = {"va": va_ref, "qh": qh_ref, "fh": fh_ref, "ih": ih_ref, "gh": gh_ref}
    cin, bias = {}, {}

    def forget_bias(z):
        log_f = jax.nn.log_sigmoid(z + fb_ref[...])
        tri = tri_ref[...]
        parts = _dot(tri, jnp.concatenate(_split_bf16(log_f, _N_SPLIT), axis=1))
        c = carry_ref[...] + sum(parts[:, i * LANES:(i + 1) * LANES] for i in range(_N_SPLIT))
        carry_ref[...] = c[tm - 1:tm, :]
        terms = jnp.concatenate(_split_bf16(c * LOG2E, _N_SPLIT), axis=1)
        bias["all"] = _dot(terms, sel_ref[...]) + one_ref[...]

    def conv_input():
        half = D_CONV // LANES
        glu = jnp.concatenate([cin[i] * jax.nn.sigmoid(cin[half + i]) for i in range(half)], axis=1)
        hs_ref[0, 0:CONV_HALO, :] = halo_ref[...]
        hs_ref[0, CONV_HALO:CONV_HALO + tm, :] = glu
        halo_ref[...] = glu[tm - CONV_HALO:, :]
        span = tm + CONV_HALO - _SUBLANES
        for s in range(1, _SUBLANES):
            hs_ref[s, 0:span, :] = hs_ref[0, s:s + span, :]

    def conv_rows(r0):
        first = CONV_HALO - (CONV_WIDTH - 1)
        acc = jnp.zeros((_CONV_ROWS, D_CONV), F32)
        for tap in range(CONV_WIDTH):
            off = first + tap
            base = r0 + off // _SUBLANES * _SUBLANES
            acc = acc + hs_ref[off % _SUBLANES, base:base + _CONV_ROWS, :] * cw_ref[tap:tap + 1, :]
        acc = acc + cb_ref[...]
        mu = jnp.mean(acc, axis=-1, keepdims=True)
        xc = acc - mu
        var = jnp.mean(xc * xc, axis=-1, keepdims=True)
        y = xc * lax.rsqrt(var + LN_EPS) * lg_ref[...] + lb_ref[...]
        cnv_ref[0, r0:r0 + _CONV_ROWS, :] = (y * jax.nn.sigmoid(y)).astype(cnv_ref.dtype)

    def head_pair(z, pair, packed_bias, o_ref):
        for e in range(2):
            hd = 2 * pair + e
            blk = pltpu.roll(z, HEAD_DIM, axis=1) if e else z
            own_bias = pltpu.roll(packed_bias, HEAD_DIM - hd * _BIAS_STRIDE, axis=1)
            tail = jnp.where(lane < HEAD_DIM + _BIAS_LANES, own_bias, 0.0)
            o_ref[0, :, hd * LANES:(hd + 1) * LANES] = jnp.where(lane < HEAD_DIM, blk, tail).astype(o_ref.dtype)

    def route(name, i, z):
        if name in direct:
            direct[name][0, :, i * LANES:(i + 1) * LANES] = z.astype(direct[name].dtype)
        elif name == "fa":
            forget_bias(z)
        elif name == "cin":
            cin[i] = z
            if len(cin) == 2 * D_CONV // LANES:
                conv_input()
        elif name == "qa":
            head_pair(z * (HEAD_DIM ** -0.5 * LOG2E), i, bias["all"][:, :LANES], qx_ref)
        elif name == "ka":
            head_pair(z, i, bias["all"][:, LANES:], kx_ref)

    conv_starts = list(range(0, tm, _CONV_ROWS))
    conv_slots = len(_IN_CHUNK_ORDER) - _IN_CHUNK_ORDER.index(3) - 1
    done = 0
    for pos, chunk in enumerate(_IN_CHUNK_ORDER):
        z = _dot(h, w_ref[:, chunk * MXU_COLS:(chunk + 1) * MXU_COLS])
        for e in range(MXU_COLS // LANES):
            name, i = _IN_TILES[chunk * (MXU_COLS // LANES) + e]
            route(name, i, z[:, e * LANES:(e + 1) * LANES])
        slot = pos - (len(_IN_CHUNK_ORDER) - conv_slots)
        if slot >= 0:
            upto = -(-(slot + 1) * len(conv_starts) // conv_slots)
            for r0 in conv_starts[done:upto]:
                conv_rows(r0)
            done = upto


def _inproj(x, g, w, fb, tri, sel, one, cw, cb, ln_g, ln_b):
    b, t, d = x.shape
    tm = tri.shape[0]
    tile = lambda n: pl.BlockSpec((1, tm, n), lambda i, j: (i, j, 0))
    consts = (g, w, fb, tri, sel, one, cw, cb, ln_g, ln_b)
    return pl.pallas_call(
        _inproj_kernel,
        grid=(b, t // tm),
        in_specs=[tile(d)] + [_resident(a.shape) for a in consts],
        out_specs=[tile(n) for _, n, _ in _IN_OUTPUTS],
        out_shape=[jax.ShapeDtypeStruct((b, t, n), dt) for _, n, dt in _IN_OUTPUTS],
        scratch_shapes=[pltpu.VMEM((1, LANES), F32), pltpu.VMEM((CONV_HALO, D_CONV), F32),
                        pltpu.VMEM((_SUBLANES, tm + CONV_HALO, D_CONV), F32)],
        compiler_params=_compiler_params(("parallel", "arbitrary")),
        name="inproj",
    )(x, *consts)


_FIRST, _LAST, _MASKED = 1, 2, 4


def _causal_tables(t, tq, tk):
    ii, jj, fl = [], [], []
    for i in range(t // tq):
        nj = -(-((i + 1) * tq) // tk)
        for j in range(nj):
            flag = (_FIRST if j == 0 else 0) | (_LAST if j == nj - 1 else 0)
            if (j + 1) * tk - 1 > i * tq:
                flag |= _MASKED
            ii.append(i), jj.append(j), fl.append(flag)
    return (np.asarray(ii, np.int32), np.asarray(jj, np.int32), np.asarray(fl, np.int32))


def _attn_kernel(ii_ref, jj_ref, fl_ref, q_ref, k_ref, v_ref, o_ref, m_ref, acc_ref, *, tq, tk):
    step = pl.program_id(1)
    i, j, flag = ii_ref[step], jj_ref[step], fl_ref[step]
    lane = lax.broadcasted_iota(jnp.int32, (1, LANES), 1)

    @pl.when((flag & _FIRST) != 0)
    def _():
        m_ref[...] = jnp.full_like(m_ref, -jnp.inf)
        acc_ref[...] = jnp.zeros_like(acc_ref)

    def scores(h):
        lanes = slice(h * LANES, (h + 1) * LANES)
        return _nt_dot(q_ref[0, :, lanes], k_ref[0, :, lanes])

    def block(masked):
        if masked:
            row = i * tq + lax.broadcasted_iota(jnp.int32, (tq, tk), 0)
            col = j * tk + lax.broadcasted_iota(jnp.int32, (tq, tk), 1)
            keep = col <= row
        s_next = scores(0)
        for h in range(ATT_HEADS):
            s = s_next
            if h + 1 < ATT_HEADS:
                s_next = scores(h + 1)
            if masked:
                s = jnp.where(keep, s, -jnp.inf)
            pair = slice((h // 2) * LANES, (h // 2 + 1) * LANES)
            own = (lane < HEAD_DIM) if h % 2 == 0 else (lane >= HEAD_DIM)
            v2 = v_ref[0, :, pair]
            v1 = jnp.where(own, v2, jnp.ones_like(v2))
            m_prev = m_ref[h]
            m_new = jnp.maximum(m_prev, jnp.max(s, axis=-1, keepdims=True))
            p = jnp.exp2(s - jnp.tile(m_new, (1, tk // LANES)))
            acc_ref[h] = jnp.exp2(m_prev - m_new) * acc_ref[h] + _dot(p.astype(BF16), v1)
            m_ref[h] = m_new

    @pl.when((flag & _MASKED) != 0)
    def _():
        block(True)

    @pl.when((flag & _MASKED) == 0)
    def _():
        block(False)

    @pl.when((flag & _LAST) != 0)
    def _():
        for p in range(HEAD_PAIRS):
            a0, a1 = acc_ref[2 * p], acc_ref[2 * p + 1]
            o0 = a0 / pltpu.roll(a0, HEAD_DIM, axis=1)
            o1 = a1 / pltpu.roll(a1, HEAD_DIM, axis=1)
            o_ref[0, :, p * LANES:(p + 1) * LANES] = jnp.where(lane < HEAD_DIM, o0, o1).astype(o_ref.dtype)


def _attention(qx, kx, va, tq, tk):
    b, t, _ = va.shape
    ii, jj, fl = _causal_tables(t, tq, tk)
    q_map = lambda bb, s, ii, jj, fl: (bb, ii[s], 0)
    k_map = lambda bb, s, ii, jj, fl: (bb, jj[s], 0)
    grid_spec = pltpu.PrefetchScalarGridSpec(
        num_scalar_prefetch=3,
        grid=(b, len(ii)),
        in_specs=[pl.BlockSpec((1, tq, D_ATT_X), q_map),
                  pl.BlockSpec((1, tk, D_ATT_X), k_map),
                  pl.BlockSpec((1, tk, D_ATT), k_map)],
        out_specs=pl.BlockSpec((1, tq, D_ATT), q_map),
        scratch_shapes=[pltpu.VMEM((ATT_HEADS, tq, LANES), F32), pltpu.VMEM((ATT_HEADS, tq, LANES), F32)],
    )
    return pl.pallas_call(
        functools.partial(_attn_kernel, tq=tq, tk=tk),
        grid_spec=grid_spec,
        out_shape=jax.ShapeDtypeStruct((b, t, D_ATT), BF16),
        compiler_params=_compiler_params(("parallel", "arbitrary")),
        name="fox_attention",
    )(jnp.asarray(ii), jnp.asarray(jj), jnp.asarray(fl), qx, kx, va)


HGRN_CHUNK = 128
_HGRN_LEVELS = HGRN_CHUNK.bit_length()


def _hgrn_level_map():
    t = np.arange(HGRN_CHUNK)[:, None]
    s = np.arange(HGRN_CHUNK)[None, :]
    high_bit = np.floor(np.log2(np.maximum(t ^ s, 1))).astype(np.int32)
    level = np.where(t == s, 0, np.where(s < t, high_bit + 1, -1)).astype(np.int32)
    return np.concatenate([level, level], axis=1)


def _hgrn_kernel(q_ref, f_ref, v_ref, g_ref, lbl_ref, gn_ref, tri_ref, lvl_ref, o_ref, st_ref,
                 *, layer, chunks):
    c = HGRN_CHUNK

    @pl.when(pl.program_id(2) == 0)
    def _():
        st_ref[...] = jnp.zeros_like(st_ref)

    z = lbl_ref[...]
    e = jnp.exp(z - jnp.max(z, axis=0, keepdims=True))
    if layer > 0:
        lb = jnp.sum(e[1:layer + 1], axis=0, keepdims=True) / jnp.sum(e, axis=0, keepdims=True)
    else:
        lb = jnp.zeros((1, LANES), F32)

    lane = lax.broadcasted_iota(jnp.int32, (1, LANES), 1)
    head0 = lane < HEAD_DIM
    row = lax.broadcasted_iota(jnp.int32, (c, LANES), 0)
    level = lvl_ref[...]
    vi = lax.broadcasted_iota(jnp.int32, (LANES, LANES), 0) // HEAD_DIM
    di = lax.broadcasted_iota(jnp.int32, (LANES, LANES), 1) // HEAD_DIM
    same_head = vi == di

    def head_stack(x):
        return jnp.concatenate([jnp.where(head0, x, 0.0), jnp.where(head0, 0.0, x)], axis=0).astype(BF16)

    for ch in range(chunks):
        rows = slice(ch * c, (ch + 1) * c)
        q = q_ref[0, rows, :]
        v2 = v_ref[0, rows, :].astype(F32)
        f = lb + (1.0 - lb) * jax.nn.sigmoid(f_ref[0, rows, :])
        kk = 1.0 - f
        g = jnp.log(f)
        b2 = _dot(tri_ref[...], jnp.concatenate(_split_bf16(g, 2), axis=1))
        b_incl = b2[:, :LANES] + b2[:, LANES:]

        kk_stack = head_stack(kk)
        a = jnp.zeros((c, 2 * c), F32)
        seg_end = b_incl
        for lvl in range(_HGRN_LEVELS):
            m = 1 << max(lvl - 1, 0)
            if lvl == 0:
                qt, kt = q, kk_stack
            elif lvl == 1:
                qt, kt = q * jnp.exp(g), kk_stack
            else:
                half = m // 2
                seg_end = jnp.where((row & half) == 0, pltpu.roll(seg_end, c - half, axis=0), seg_end)
                prev_end = jnp.where(row >= m, pltpu.roll(seg_end, m, axis=0), 0.0)
                qt = q * jnp.exp(b_incl - prev_end)
                kt = head_stack(kk * jnp.exp(seg_end - b_incl))
            a = jnp.where(level == lvl, _nt_dot(qt.astype(BF16), kt), a)
        o = _dot(a.astype(BF16), head_stack(v2))
        st = st_ref[...]
        o = o + _nt_dot((q * jnp.exp(b_incl)).astype(BF16), st.astype(BF16))
        b_end = b_incl[c - 1:c, :]
        upd = _dot(v2.T.astype(BF16), (kk * jnp.exp(b_end - b_incl)).astype(BF16))
        st_ref[...] = st * jnp.exp(b_end) + jnp.where(same_head, upd, 0.0)

        o2 = o * o
        ms0 = jnp.sum(jnp.where(head0, o2, 0.0), axis=-1, keepdims=True) * (1.0 / HEAD_DIM)
        ms1 = jnp.sum(jnp.where(head0, 0.0, o2), axis=-1, keepdims=True) * (1.0 / HEAD_DIM)
        inv = jnp.where(head0, lax.rsqrt(ms0 + RMS_EPS), lax.rsqrt(ms1 + RMS_EPS))
        gate = g_ref[0, rows, :]
        o_ref[0, rows, :] = (o * inv * gn_ref[...] * (gate * jax.nn.sigmoid(gate))).astype(o_ref.dtype)


def _hgrn(qh, fh, ih, gh, lb_logits, gn, tri, lvl, layer, tt):
    b, t, _ = qh.shape
    tile = pl.BlockSpec((1, tt, LANES), lambda i, p, j: (i, j, p))
    return pl.pallas_call(
        functools.partial(_hgrn_kernel, layer=layer, chunks=tt // HGRN_CHUNK),
        grid=(b, HEAD_PAIRS, t // tt),
        in_specs=[tile, tile, tile, tile,
                  pl.BlockSpec((lb_logits.shape[0], LANES), lambda i, p, j: (0, p)),
                  _resident(gn.shape), _resident(tri.shape), _resident(lvl.shape)],
        out_specs=tile,
        out_shape=jax.ShapeDtypeStruct((b, t, D_HGRN), BF16),
        scratch_shapes=[pltpu.VMEM((LANES, LANES), F32)],
        compiler_params=_compiler_params(("parallel", "parallel", "arbitrary")),
        name="hgrn2",
    )(qh, fh, ih, gh, lb_logits, gn, tri, lvl)


_FFN_COLS = 256


def _ffn_kernel(x_ref, att_ref, cnv_ref, hg_ref, wo_ref, g_ref, wi_ref, w2_ref, gf_ref, o_ref, a_ref,
                *, d_ff, final):
    mix = jnp.concatenate([att_ref[...], cnv_ref[...], hg_ref[...]], axis=1)
    x1 = x_ref[...] + _dot(mix, wo_ref[...])
    h = _rmsnorm_rows(x1, g_ref[...]).astype(BF16)
    for c0 in range(0, d_ff, _FFN_COLS):
        gate = _dot(h, wi_ref[:, c0:c0 + _FFN_COLS])
        up = _dot(h, wi_ref[:, d_ff + c0:d_ff + c0 + _FFN_COLS])
        a_ref[:, c0:c0 + _FFN_COLS] = (gate * jax.nn.sigmoid(gate) * up).astype(BF16)
    y = x1 + _dot(a_ref[...], w2_ref[...])
    if final:
        y = _rmsnorm_rows(y, gf_ref[...])
    o_ref[...] = y


def _out_ffn(x2d, att, cnv, hg, wo, g, wi, w2, gf, tm, final):
    m, d = x2d.shape
    d_ff = w2.shape[0]
    row = lambda n: pl.BlockSpec((tm, n), lambda i: (i, 0))
    return pl.pallas_call(
        functools.partial(_ffn_kernel, d_ff=d_ff, final=final),
        grid=(m // tm,),
        in_specs=[row(d), row(D_ATT), row(D_CONV), row(D_HGRN), _resident(wo.shape), _resident(g.shape),
                  _resident(wi.shape), _resident(w2.shape), _resident(gf.shape)],
        out_specs=row(d),
        out_shape=jax.ShapeDtypeStruct((m, d), F32),
        scratch_shapes=[pltpu.VMEM((tm, d_ff), BF16)],
        compiler_params=_compiler_params(("parallel",)),
        name="out_ffn",
    )(x2d, att, cnv, hg, wo, g, wi, w2, gf)


def _tiles(t):
    return dict(tm=min(512, t), tq=min(512, t), tk=min(512, t), tt=min(256, t))


_W_IN_SIZES = (D_ATT, D_ATT, D_ATT, ATT_HEADS, 2 * D_CONV, D_HGRN, D_HGRN, D_HGRN, D_HGRN)
_W_IN_AFTER_F = 3 * D_ATT + ATT_HEADS
_W_IN_TAIL = sum(_W_IN_SIZES[4:])


def _w_in_layout_kernel(w_ref, o_ref):
    x = w_ref[0]
    lane = lax.broadcasted_iota(jnp.int32, (1, LANES), 1)
    zero = jnp.zeros((x.shape[0], LANES), F32)

    f_tile = jnp.where(lane < ATT_HEADS, x[:, 3 * D_ATT:3 * D_ATT + LANES], zero)
    shift = _W_IN_AFTER_F % LANES
    tail0 = _W_IN_AFTER_F - shift
    tail = pltpu.roll(x, x.shape[1] - shift, axis=1)[:, tail0:tail0 + _W_IN_TAIL]
    o_ref[0] = jnp.concatenate([x[:, 2 * D_ATT:3 * D_ATT], f_tile, tail, x[:, :2 * D_ATT]], axis=1).astype(o_ref.dtype)


def _w_in_layout(w_in):
    depth, d, d_in = w_in.shape
    rows = min(256, d)
    width = -(-d_in // LANES) * LANES
    d_out = sum(n for _, n in _IN_COLS)
    return pl.pallas_call(
        _w_in_layout_kernel,
        grid=(depth, d // rows),
        in_specs=[pl.BlockSpec((1, rows, width), lambda l, i: (l, i, 0))],
        out_specs=pl.BlockSpec((1, rows, d_out), lambda l, i: (l, i, 0)),
        out_shape=jax.ShapeDtypeStruct((depth, d, d_out), BF16),
        compiler_params=_compiler_params(("parallel", "parallel")),
        name="w_in_layout",
    )(w_in)


def kernel(x, norm_mix_g, w_in, fgate_b, conv_w, conv_b, conv_ln_g, conv_ln_b, hgrn_lb_logits, hgrn_norm_g,
           w_out, norm_ffn_g, w_ffn_in, w_ffn_out, norm_final_g):
    b, t, d = x.shape
    depth = w_in.shape[0]
    tl = _tiles(t)
    row = lambda a: a.reshape(1, -1).astype(F32)

    tri = jnp.asarray(np.tril(np.ones((tl["tm"], tl["tm"]), np.float32)), BF16)
    sel, one = _bias_lane_constants()
    sel, one = jnp.asarray(sel, BF16), jnp.asarray(one)
    tri_h = jnp.asarray(np.tril(np.ones((HGRN_CHUNK, HGRN_CHUNK), np.float32)), BF16)
    lvl = jnp.asarray(_hgrn_level_map())
    lb_logits = hgrn_lb_logits.astype(F32)
    gf = row(norm_final_g)

    w_all = _w_in_layout(w_in)
    for l in range(depth):
        w = w_all[l]
        fb = row(jnp.pad(fgate_b[l], (0, LANES - ATT_HEADS)))
        cw = jnp.concatenate([conv_w[l], jnp.zeros((1, D_CONV), F32)], axis=0)
        qx, kx, va, cnv, qh, fh, ih, gh = _inproj(x, row(norm_mix_g[l]), w, fb, tri, sel, one, cw,
                                                  row(conv_b[l]), row(conv_ln_g[l]), row(conv_ln_b[l]))

        att = _attention(qx, kx, va, tl["tq"], tl["tk"])

        gn = row(jnp.tile(hgrn_norm_g[l], LANES // HEAD_DIM))
        hg = _hgrn(qh, fh, ih, gh, lb_logits, gn, tri_h, lvl, l, tl["tt"])

        x = _out_ffn(x.reshape(b * t, d), att.reshape(b * t, -1), cnv.reshape(b * t, -1), hg.reshape(b * t, -1),
                     w_out[l].astype(BF16), row(norm_ffn_g[l]), w_ffn_in[l].astype(BF16),
                     w_ffn_out[l].astype(BF16), gf, tl["tm"], final=(l == depth - 1)).reshape(b, t, d)
    return x
```

```python
import functools

import numpy as np
import jax
import jax.numpy as jnp
from jax import lax
from jax.experimental import pallas as pl
from jax.experimental.pallas import tpu as pltpu

F32 = jnp.float32
BF16 = jnp.bfloat16

HEAD_DIM = 64
ATT_HEADS = 6
D_ATT = ATT_HEADS * HEAD_DIM
D_CONV = 256
CONV_WIDTH = 31
HGRN_HEADS = 6
D_HGRN = HGRN_HEADS * HEAD_DIM
RMS_EPS = 1e-6
LN_EPS = 1e-5

LANES = 128
HEAD_PAIRS = D_ATT // LANES
CONV_HALO = 32
VMEM_LIMIT_BYTES = 56 * 1024 * 1024

D_ATT_X = ATT_HEADS * LANES
LOG2E = 1.4426950408889634

_IN_COLS = (("va", D_ATT), ("fa", LANES), ("cin", 2 * D_CONV), ("qh", D_HGRN), ("fh", D_HGRN),
            ("ih", D_HGRN), ("gh", D_HGRN), ("qa", D_ATT), ("ka", D_ATT))
_IN_TILES = tuple((name, i) for name, n in _IN_COLS for i in range(n // LANES))
MXU_COLS = 2 * LANES
_IN_CHUNK_ORDER = (1, 2, 3, 0) + tuple(range(4, len(_IN_TILES) * LANES // MXU_COLS))
_IN_OUTPUTS = (("qx", D_ATT_X, BF16), ("kx", D_ATT_X, BF16), ("va", D_ATT, BF16), ("cnv", D_CONV, BF16),
               ("qh", D_HGRN, F32), ("fh", D_HGRN, F32), ("ih", D_HGRN, BF16), ("gh", D_HGRN, F32))
_N_SPLIT = 3


def _nt_dot(a, b):
    return lax.dot_general(a, b, (((1,), (1,)), ((), ())), preferred_element_type=F32)


def _dot(a, b):
    return jnp.dot(a, b, preferred_element_type=F32)


def _rmsnorm_rows(x, g):
    return x * lax.rsqrt(jnp.mean(x * x, axis=-1, keepdims=True) + RMS_EPS) * g


def _compiler_params(semantics):
    return pltpu.CompilerParams(dimension_semantics=semantics, vmem_limit_bytes=VMEM_LIMIT_BYTES)


def _resident(shape):
    nd = len(shape)
    return pl.BlockSpec(shape, lambda *_: (0,) * nd, pipeline_mode=pl.Buffered(1))


def _split_bf16(x, n):
    terms = []
    for _ in range(n - 1):
        t = x.astype(BF16)
        terms.append(t)
        x = x - t.astype(F32)
    terms.append(x.astype(BF16))
    return terms


_BIAS_LANES = 2 * _N_SPLIT
_BIAS_STRIDE = 8


def _bias_lane_constants():
    sel = np.zeros((_N_SPLIT * LANES, 2 * LANES), np.float32)
    one = np.zeros((1, 2 * LANES), np.float32)
    for h in range(ATT_HEADS):
        base = h * _BIAS_STRIDE
        for i in range(_N_SPLIT):
            sel[i * LANES + h, base + i] = 1.0
            one[0, base + _N_SPLIT + i] = 1.0
            one[0, LANES + base + i] = 1.0
            sel[i * LANES + h, LANES + base + _N_SPLIT + i] = -1.0
    return sel, one


_CONV_ROWS = 32
_SUBLANES = 8


def _inproj_kernel(x_ref, g_ref, w_ref, fb_ref, tri_ref, sel_ref, one_ref, cw_ref, cb_ref, lg_ref, lb_ref,
                   qx_ref, kx_ref, va_ref, cnv_ref, qh_ref, fh_ref, ih_ref, gh_ref,
                   carry_ref, halo_ref, hs_ref):
    tm = x_ref.shape[1]

    @pl.when(pl.program_id(1) == 0)
    def _():
        carry_ref[...] = jnp.zeros_like(carry_ref)
        halo_ref[...] = jnp.zeros_like(halo_ref)

    h = _rmsnorm_rows(x_ref[0], g_ref[...]).astype(BF16)
    lane = lax.broadcasted_iota(jnp.int32, (1, LANES), 1)
    direct = {"va": va_ref, "qh": qh_ref, "fh": fh_ref, "ih": ih_ref, "gh": gh_ref}
    cin, bias = {}, {}

    def forget_bias(z):
        log_f = jax.nn.log_sigmoid(z + fb_ref[...])
        tri = tri_ref[...]
        parts = _dot(tri, jnp.concatenate(_split_bf16(log_f, _N_SPLIT), axis=1))
        c = carry_ref[...] + sum(parts[:, i * LANES:(i + 1) * LANES] for i in range(_N_SPLIT))
        carry_ref[...] = c[tm - 1:tm, :]
        terms = jnp.concatenate(_split_bf16(c * LOG2E, _N_SPLIT), axis=1)
        bias["all"] = _dot(terms, sel_ref[...]) + one_ref[...]

    def conv_input():
        half = D_CONV // LANES
        glu = jnp.concatenate([cin[i] * jax.nn.sigmoid(cin[half + i]) for i in range(half)], axis=1)
        hs_ref[0, 0:CONV_HALO, :] = halo_ref[...]
        hs_ref[0, CONV_HALO:CONV_HALO + tm, :] = glu
        halo_ref[...] = glu[tm - CONV_HALO:, :]
        span = tm + CONV_HALO - _SUBLANES
        for s in range(1, _SUBLANES):
            hs_ref[s, 0:span, :] = hs_ref[0, s:s + span, :]

    def conv_rows(r0):
        first = CONV_HALO - (CONV_WIDTH - 1)
        acc = jnp.zeros((_CONV_ROWS, D_CONV), F32)
        for tap in range(CONV_WIDTH):
            off = first + tap
            base = r0 + off // _SUBLANES * _SUBLANES
            acc = acc + hs_ref[off % _SUBLANES, base:base + _CONV_ROWS, :] * cw_ref[tap:tap + 1, :]
        acc = acc + cb_ref[...]
        mu = jnp.mean(acc, axis=-1, keepdims=True)
        xc = acc - mu
        var = jnp.mean(xc * xc, axis=-1, keepdims=True)
        y = xc * lax.rsqrt(var + LN_EPS) * lg_ref[...] + lb_ref[...]
        cnv_ref[0, r0:r0 + _CONV_ROWS, :] = (y * jax.nn.sigmoid(y)).astype(cnv_ref.dtype)

    def head_pair(z, pair, packed_bias, o_ref):
        for e in range(2):
            hd = 2 * pair + e
            blk = pltpu.roll(z, HEAD_DIM, axis=1) if e else z
            own_bias = pltpu.roll(packed_bias, HEAD_DIM - hd * _BIAS_STRIDE, axis=1)
            tail = jnp.where(lane < HEAD_DIM + _BIAS_LANES, own_bias, 0.0)
            o_ref[0, :, hd * LANES:(hd + 1) * LANES] = jnp.where(lane < HEAD_DIM, blk, tail).astype(o_ref.dtype)

    def route(name, i, z):
        if name in direct:
            direct[name][0, :, i * LANES:(i + 1) * LANES] = z.astype(direct[name].dtype)
        elif name == "fa":
            forget_bias(z)
        elif name == "cin":
            cin[i] = z
            if len(cin) == 2 * D_CONV // LANES:
                conv_input()
        elif name == "qa":
            head_pair(z * (HEAD_DIM ** -0.5 * LOG2E), i, bias["all"][:, :LANES], qx_ref)
        elif name == "ka":
            head_pair(z, i, bias["all"][:, LANES:], kx_ref)

    conv_starts = list(range(0, tm, _CONV_ROWS))
    conv_slots = len(_IN_CHUNK_ORDER) - _IN_CHUNK_ORDER.index(3) - 1
    done = 0
    for pos, chunk in enumerate(_IN_CHUNK_ORDER):
        z = _dot(h, w_ref[:, chunk * MXU_COLS:(chunk + 1) * MXU_COLS])
        for e in range(MXU_COLS // LANES):
            name, i = _IN_TILES[chunk * (MXU_COLS // LANES) + e]
            route(name, i, z[:, e * LANES:(e + 1) * LANES])
        slot = pos - (len(_IN_CHUNK_ORDER) - conv_slots)
        if slot >= 0:
            upto = -(-(slot + 1) * len(conv_starts) // conv_slots)
            for r0 in conv_starts[done:upto]:
                conv_rows(r0)
            done = upto


def _inproj(x, g, w, fb, tri, sel, one, cw, cb, ln_g, ln_b):
    b, t, d = x.shape
    tm = tri.shape[0]
    tile = lambda n: pl.BlockSpec((1, tm, n), lambda i, j: (i, j, 0))
    consts = (g, w, fb, tri, sel, one, cw, cb, ln_g, ln_b)
    return pl.pallas_call(
        _inproj_kernel,
        grid=(b, t // tm),
        in_specs=[tile(d)] + [_resident(a.shape) for a in consts],
        out_specs=[tile(n) for _, n, _ in _IN_OUTPUTS],
        out_shape=[jax.ShapeDtypeStruct((b, t, n), dt) for _, n, dt in _IN_OUTPUTS],
        scratch_shapes=[pltpu.VMEM((1, LANES), F32), pltpu.VMEM((CONV_HALO, D_CONV), F32),
                        pltpu.VMEM((_SUBLANES, tm + CONV_HALO, D_CONV), F32)],
        compiler_params=_compiler_params(("parallel", "arbitrary")),
        name="inproj",
    )(x, *consts)


_FIRST, _LAST, _MASKED = 1, 2, 4


def _causal_tables(t, tq, tk):
    ii, jj, fl = [], [], []
    for i in range(t // tq):
        nj = -(-((i + 1) * tq) // tk)
        for j in range(nj):
            flag = (_FIRST if j == 0 else 0) | (_LAST if j == nj - 1 else 0)
            if (j + 1) * tk - 1 > i * tq:
                flag |= _MASKED
            ii.append(i), jj.append(j), fl.append(flag)
    return (np.asarray(ii, np.int32), np.asarray(jj, np.int32), np.asarray(fl, np.int32))


def _attn_kernel(ii_ref, jj_ref, fl_ref, q_ref, k_ref, v_ref, o_ref, m_ref, acc_ref, *, tq, tk):
    step = pl.program_id(1)
    i, j, flag = ii_ref[step], jj_ref[step], fl_ref[step]
    lane = lax.broadcasted_iota(jnp.int32, (1, LANES), 1)

    @pl.when((flag & _FIRST) != 0)
    def _():
        m_ref[...] = jnp.full_like(m_ref, -jnp.inf)
        acc_ref[...] = jnp.zeros_like(acc_ref)

    def scores(h):
        lanes = slice(h * LANES, (h + 1) * LANES)
        return _nt_dot(q_ref[0, :, lanes], k_ref[0, :, lanes])

    def block(masked):
        if masked:
            row = i * tq + lax.broadcasted_iota(jnp.int32, (tq, tk), 0)
            col = j * tk + lax.broadcasted_iota(jnp.int32, (tq, tk), 1)
            keep = col <= row
        s_next = scores(0)
        for h in range(ATT_HEADS):
            s = s_next
            if h + 1 < ATT_HEADS:
                s_next = scores(h + 1)
            if masked:
                s = jnp.where(keep, s, -jnp.inf)
            pair = slice((h // 2) * LANES, (h // 2 + 1) * LANES)
            own = (lane < HEAD_DIM) if h % 2 == 0 else (lane >= HEAD_DIM)
            v2 = v_ref[0, :, pair]
            v1 = jnp.where(own, v2, jnp.ones_like(v2))
            m_prev = m_ref[h]
            m_new = jnp.maximum(m_prev, jnp.max(s, axis=-1, keepdims=True))
            p = jnp.exp2(s - jnp.tile(m_new, (1, tk // LANES)))
            acc_ref[h] = jnp.exp2(m_prev - m_new) * acc_ref[h] + _dot(p.astype(BF16), v1)
            m_ref[h] = m_new

    @pl.when((flag & _MASKED) != 0)
    def _():
        block(True)

    @pl.when((flag & _MASKED) == 0)
    def _():
        block(False)

    @pl.when((flag & _LAST) != 0)
    def _():
        for p in range(HEAD_PAIRS):
            a0, a1 = acc_ref[2 * p], acc_ref[2 * p + 1]
            o0 = a0 / pltpu.roll(a0, HEAD_DIM, axis=1)
            o1 = a1 / pltpu.roll(a1, HEAD_DIM, axis=1)
            o_ref[0, :, p * LANES:(p + 1) * LANES] = jnp.where(lane < HEAD_DIM, o0, o1).astype(o_ref.dtype)


def _attention(qx, kx, va, tq, tk):
    b, t, _ = va.shape
    ii, jj, fl = _causal_tables(t, tq, tk)
    q_map = lambda bb, s, ii, jj, fl: (bb, ii[s], 0)
    k_map = lambda bb, s, ii, jj, fl: (bb, jj[s], 0)
    grid_spec = pltpu.PrefetchScalarGridSpec(
        num_scalar_prefetch=3,
        grid=(b, len(ii)),
        in_specs=[pl.BlockSpec((1, tq, D_ATT_X), q_map),
                  pl.BlockSpec((1, tk, D_ATT_X), k_map),
                  pl.BlockSpec((1, tk, D_ATT), k_map)],
        out_specs=pl.BlockSpec((1, tq, D_ATT), q_map),
        scratch_shapes=[pltpu.VMEM((ATT_HEADS, tq, LANES), F32), pltpu.VMEM((ATT_HEADS, tq, LANES), F32)],
    )
    return pl.pallas_call(
        functools.partial(_attn_kernel, tq=tq, tk=tk),
        grid_spec=grid_spec,
        out_shape=jax.ShapeDtypeStruct((b, t, D_ATT), BF16),
        compiler_params=_compiler_params(("parallel", "arbitrary")),
        name="fox_attention",
    )(jnp.asarray(ii), jnp.asarray(jj), jnp.asarray(fl), qx, kx, va)


HGRN_CHUNK = 128
_HGRN_LEVELS = HGRN_CHUNK.bit_length()


def _hgrn_level_map():
    t = np.arange(HGRN_CHUNK)[:, None]
    s = np.arange(HGRN_CHUNK)[None, :]
    high_bit = np.floor(np.log2(np.maximum(t ^ s, 1))).astype(np.int32)
    level = np.where(t == s, 0, np.where(s < t, high_bit + 1, -1)).astype(np.int32)
    return np.concatenate([level, level], axis=1)


def _hgrn_kernel(q_ref, f_ref, v_ref, g_ref, lbl_ref, gn_ref, tri_ref, lvl_ref, o_ref, st_ref,
                 *, layer, chunks):
    c = HGRN_CHUNK

    @pl.when(pl.program_id(1) == 0)
    def _():
        st_ref[...] = jnp.zeros_like(st_ref)

    z = lbl_ref[...]
    e = jnp.exp(z - jnp.max(z, axis=0, keepdims=True))
    if layer > 0:
        lb_all = jnp.sum(e[1:layer + 1], axis=0, keepdims=True) / jnp.sum(e, axis=0, keepdims=True)
    else:
        lb_all = jnp.zeros((1, D_HGRN), F32)

    lane = lax.broadcasted_iota(jnp.int32, (1, LANES), 1)
    head0 = lane < HEAD_DIM
    row = lax.broadcasted_iota(jnp.int32, (c, LANES), 0)
    level = lvl_ref[...]
    vi = lax.broadcasted_iota(jnp.int32, (LANES, LANES), 0) // HEAD_DIM
    di = lax.broadcasted_iota(jnp.int32, (LANES, LANES), 1) // HEAD_DIM
    same_head = vi == di

    def head_split(x):
        return jnp.where(head0, x, 0.0), jnp.where(head0, 0.0, x)

    def head_stack(x0, x1):
        return jnp.concatenate([x0, x1], axis=0).astype(BF16)

    def start(rows, p):
        cols = slice(p * LANES, (p + 1) * LANES)
        lb = lb_all[:, cols]
        q = q_ref[0, rows, cols]
        v2 = v_ref[0, rows, cols].astype(F32)
        f = lb + (1.0 - lb) * jax.nn.sigmoid(f_ref[0, rows, cols])
        kk = 1.0 - f
        g = jnp.log(f) * LOG2E
        b2 = _dot(tri_ref[...], jnp.concatenate(_split_bf16(g, 2), axis=1))
        b_incl = b2[:, :LANES] + b2[:, LANES:]
        kk0, kk1 = head_split(kk)
        return dict(cols=cols, p=p, q=q, v2=v2, kk=kk, kk0=kk0, kk1=kk1, g=g, b_incl=b_incl,
                    kk_stack=head_stack(kk0, kk1),
                    a=jnp.zeros((c, 2 * c), F32), seg_end=b_incl)

    def level_step(s, lvl):
        q, kk, b_incl = s["q"], s["kk"], s["b_incl"]
        m = 1 << max(lvl - 1, 0)
        if lvl == 0:
            qt, kt = q, s["kk_stack"]
        elif lvl == 1:
            qt, kt = q * jnp.exp2(s["g"]), s["kk_stack"]
        else:
            half = m // 2
            seg_end = s["seg_end"]
            seg_end = jnp.where((row & half) == 0, pltpu.roll(seg_end, c - half, axis=0), seg_end)
            prev_end = jnp.where(row >= m, pltpu.roll(seg_end, m, axis=0), 0.0)
            qt = q * jnp.exp2(b_incl - prev_end)
            later = jnp.exp2(seg_end - b_incl)
            kt = head_stack(s["kk0"] * later, s["kk1"] * later)
            s["seg_end"] = seg_end
        s["a"] = jnp.where(level == lvl, _nt_dot(qt.astype(BF16), kt), s["a"])

    def finish(s, rows):
        q, kk, v2, b_incl, p, cols = s["q"], s["kk"], s["v2"], s["b_incl"], s["p"], s["cols"]
        o = _dot(s["a"].astype(BF16), head_stack(*head_split(v2)))
        st = st_ref[p]
        o = o + _nt_dot((q * jnp.exp2(b_incl)).astype(BF16), st.astype(BF16))
        b_end = b_incl[c - 1:c, :]
        upd = _dot(v2.T.astype(BF16), (kk * jnp.exp2(b_end - b_incl)).astype(BF16))
        st_ref[p] = st * jnp.exp2(b_end) + jnp.where(same_head, upd, 0.0)

        o2 = o * o
        ms0 = jnp.sum(jnp.where(head0, o2, 0.0), axis=-1, keepdims=True) * (1.0 / HEAD_DIM)
        ms1 = jnp.sum(jnp.where(head0, 0.0, o2), axis=-1, keepdims=True) * (1.0 / HEAD_DIM)
        inv = jnp.where(head0, lax.rsqrt(ms0 + RMS_EPS), lax.rsqrt(ms1 + RMS_EPS))
        gate = g_ref[0, rows, cols]
        o_ref[0, rows, cols] = (o * inv * gn_ref[...] * (gate * jax.nn.sigmoid(gate))).astype(o_ref.dtype)

    for ch in range(chunks):
        rows = slice(ch * c, (ch + 1) * c)
        states = [start(rows, p) for p in range(HEAD_PAIRS)]
        for lvl in range(_HGRN_LEVELS):
            for s in states:
                level_step(s, lvl)
        for s in states:
            finish(s, rows)


def _hgrn(qh, fh, ih, gh, lb_logits, gn, tri, lvl, layer, tt):
    b, t, _ = qh.shape
    tile = pl.BlockSpec((1, tt, D_HGRN), lambda i, j: (i, j, 0))
    return pl.pallas_call(
        functools.partial(_hgrn_kernel, layer=layer, chunks=tt // HGRN_CHUNK),
        grid=(b, t // tt),
        in_specs=[tile, tile, tile, tile, _resident(lb_logits.shape),
                  _resident(gn.shape), _resident(tri.shape), _resident(lvl.shape)],
        out_specs=tile,
        out_shape=jax.ShapeDtypeStruct((b, t, D_HGRN), BF16),
        scratch_shapes=[pltpu.VMEM((HEAD_PAIRS, LANES, LANES), F32)],
        compiler_params=_compiler_params(("parallel", "arbitrary")),
        name="hgrn2",
    )(qh, fh, ih, gh, lb_logits, gn, tri, lvl)


_FFN_COLS = 256


def _ffn_kernel(x_ref, att_ref, cnv_ref, hg_ref, wo_ref, g_ref, wi_ref, w2_ref, gf_ref, o_ref, a_ref,
                *, d_ff, final):
    mix = jnp.concatenate([att_ref[...], cnv_ref[...], hg_ref[...]], axis=1)
    x1 = x_ref[...] + _dot(mix, wo_ref[...])
    h = _rmsnorm_rows(x1, g_ref[...]).astype(BF16)
    for c0 in range(0, d_ff, _FFN_COLS):
        gate = _dot(h, wi_ref[:, c0:c0 + _FFN_COLS])
        up = _dot(h, wi_ref[:, d_ff + c0:d_ff + c0 + _FFN_COLS])
        a_ref[:, c0:c0 + _FFN_COLS] = (gate * jax.nn.sigmoid(gate) * up).astype(BF16)
    y = x1 + _dot(a_ref[...], w2_ref[...])
    if final:
        y = _rmsnorm_rows(y, gf_ref[...])
    o_ref[...] = y


def _out_ffn(x2d, att, cnv, hg, wo, g, wi, w2, gf, tm, final):
    m, d = x2d.shape
    d_ff = w2.shape[0]
    row = lambda n: pl.BlockSpec((tm, n), lambda i: (i, 0))
    return pl.pallas_call(
        functools.partial(_ffn_kernel, d_ff=d_ff, final=final),
        grid=(m // tm,),
        in_specs=[row(d), row(D_ATT), row(D_CONV), row(D_HGRN), _resident(wo.shape), _resident(g.shape),
                  _resident(wi.shape), _resident(w2.shape), _resident(gf.shape)],
        out_specs=row(d),
        out_shape=jax.ShapeDtypeStruct((m, d), F32),
        scratch_shapes=[pltpu.VMEM((tm, d_ff), BF16)],
        compiler_params=_compiler_params(("parallel",)),
        name="out_ffn",
    )(x2d, att, cnv, hg, wo, g, wi, w2, gf)


def _tiles(t):
    return dict(tm=min(512, t), tq=min(512, t), tk=min(512, t), tt=min(256, t))


_W_IN_SIZES = (D_ATT, D_ATT, D_ATT, ATT_HEADS, 2 * D_CONV, D_HGRN, D_HGRN, D_HGRN, D_HGRN)
_W_IN_AFTER_F = 3 * D_ATT + ATT_HEADS
_W_IN_TAIL = sum(_W_IN_SIZES[4:])


def _w_in_layout_kernel(w_ref, o_ref):
    x = w_ref[0]
    lane = lax.broadcasted_iota(jnp.int32, (1, LANES), 1)
    zero = jnp.zeros((x.shape[0], LANES), F32)

    f_tile = jnp.where(lane < ATT_HEADS, x[:, 3 * D_ATT:3 * D_ATT + LANES], zero)
    shift = _W_IN_AFTER_F % LANES
    tail0 = _W_IN_AFTER_F - shift
    tail = pltpu.roll(x, x.shape[1] - shift, axis=1)[:, tail0:tail0 + _W_IN_TAIL]
    o_ref[0] = jnp.concatenate([x[:, 2 * D_ATT:3 * D_ATT], f_tile, tail, x[:, :2 * D_ATT]], axis=1).astype(o_ref.dtype)


def _w_in_layout(w_in):
    depth, d, d_in = w_in.shape
    rows = min(256, d)
    width = -(-d_in // LANES) * LANES
    d_out = sum(n for _, n in _IN_COLS)
    return pl.pallas_call(
        _w_in_layout_kernel,
        grid=(depth, d // rows),
        in_specs=[pl.BlockSpec((1, rows, width), lambda l, i: (l, i, 0))],
        out_specs=pl.BlockSpec((1, rows, d_out), lambda l, i: (l, i, 0)),
        out_shape=jax.ShapeDtypeStruct((depth, d, d_out), BF16),
        compiler_params=_compiler_params(("parallel", "parallel")),
        name="w_in_layout",
    )(w_in)


def kernel(x, norm_mix_g, w_in, fgate_b, conv_w, conv_b, conv_ln_g, conv_ln_b, hgrn_lb_logits, hgrn_norm_g,
           w_out, norm_ffn_g, w_ffn_in, w_ffn_out, norm_final_g):
    b, t, d = x.shape
    depth = w_in.shape[0]
    tl = _tiles(t)
    row = lambda a: a.reshape(1, -1).astype(F32)

    tri = jnp.asarray(np.tril(np.ones((tl["tm"], tl["tm"]), np.float32)), BF16)
    sel, one = _bias_lane_constants()
    sel, one = jnp.asarray(sel, BF16), jnp.asarray(one)
    tri_h = jnp.asarray(np.tril(np.ones((HGRN_CHUNK, HGRN_CHUNK), np.float32)), BF16)
    lvl = jnp.asarray(_hgrn_level_map())
    lb_logits = hgrn_lb_logits.astype(F32)
    gf = row(norm_final_g)

    w_all = _w_in_layout(w_in)
    for l in range(depth):
        w = w_all[l]
        fb = row(jnp.pad(fgate_b[l], (0, LANES - ATT_HEADS)))
        cw = jnp.concatenate([conv_w[l], jnp.zeros((1, D_CONV), F32)], axis=0)
        qx, kx, va, cnv, qh, fh, ih, gh = _inproj(x, row(norm_mix_g[l]), w, fb, tri, sel, one, cw,
                                                  row(conv_b[l]), row(conv_ln_g[l]), row(conv_ln_b[l]))

        att = _attention(qx, kx, va, tl["tq"], tl["tk"])

        gn = row(jnp.tile(hgrn_norm_g[l], LANES // HEAD_DIM))
        hg = _hgrn(qh, fh, ih, gh, lb_logits, gn, tri_h, lvl, l, tl["tt"])

        x = _out_ffn(x.reshape(b * t, d), att.reshape(b * t, -1), cnv.reshape(b * t, -1), hg.reshape(b * t, -1),
                     w_out[l].astype(BF16), row(norm_ffn_g[l]), w_ffn_in[l].astype(BF16),
                     w_ffn_out[l].astype(BF16), gf, tl["tm"], final=(l == depth - 1)).reshape(b, t, d)
    return x
```

```python
import functools

import numpy as np
import jax
import jax.numpy as jnp
from jax import lax
from jax.experimental import pallas as pl
from jax.experimental.pallas import tpu as pltpu

F32 = jnp.float32
BF16 = jnp.bfloat16

HEAD_DIM = 64
ATT_HEADS = 6
D_ATT = ATT_HEADS * HEAD_DIM
D_CONV = 256
CONV_WIDTH = 31
HGRN_HEADS = 6
D_HGRN = HGRN_HEADS * HEAD_DIM
RMS_EPS = 1e-6
LN_EPS = 1e-5

LANES = 128
HEAD_PAIRS = D_ATT // LANES
CONV_HALO = 32
VMEM_LIMIT_BYTES = 56 * 1024 * 1024

D_ATT_X = ATT_HEADS * LANES
LOG2E = 1.4426950408889634

_IN_COLS = (("va", D_ATT), ("fa", LANES), ("cin", 2 * D_CONV), ("qh", D_HGRN), ("fh", D_HGRN),
            ("ih", D_HGRN), ("gh", D_HGRN), ("qa", D_ATT), ("ka", D_ATT))
_IN_TILES = tuple((name, i) for name, n in _IN_COLS for i in range(n // LANES))
MXU_COLS = 2 * LANES
_IN_CHUNK_ORDER = (2, 3, 1, 0) + tuple(range(4, len(_IN_TILES) * LANES // MXU_COLS))
_IN_OUTPUTS = (("qx", D_ATT_X, BF16), ("kx", D_ATT_X, BF16), ("va", D_ATT, BF16), ("cnv", D_CONV, BF16),
               ("qh", D_HGRN, F32), ("fh", D_HGRN, F32), ("ih", D_HGRN, BF16), ("gh", D_HGRN, F32))
_N_SPLIT = 3


def _nt_dot(a, b):
    return lax.dot_general(a, b, (((1,), (1,)), ((), ())), preferred_element_type=F32)


def _dot(a, b):
    return jnp.dot(a, b, preferred_element_type=F32)


def _rmsnorm_rows(x, g):
    return x * lax.rsqrt(jnp.mean(x * x, axis=-1, keepdims=True) + RMS_EPS) * g


def _compiler_params(semantics):
    return pltpu.CompilerParams(dimension_semantics=semantics, vmem_limit_bytes=VMEM_LIMIT_BYTES)


def _resident(shape):
    nd = len(shape)
    return pl.BlockSpec(shape, lambda *_: (0,) * nd, pipeline_mode=pl.Buffered(1))


def _split_bf16(x, n):
    terms = []
    for _ in range(n - 1):
        t = x.astype(BF16)
        terms.append(t)
        x = x - t.astype(F32)
    terms.append(x.astype(BF16))
    return terms


_BIAS_LANES = 2 * _N_SPLIT
_BIAS_STRIDE = 8


def _bias_lane_constants():
    sel = np.zeros((_N_SPLIT * LANES, 2 * LANES), np.float32)
    one = np.zeros((1, 2 * LANES), np.float32)
    for h in range(ATT_HEADS):
        base = h * _BIAS_STRIDE
        for i in range(_N_SPLIT):
            sel[i * LANES + h, base + i] = 1.0
            one[0, base + _N_SPLIT + i] = 1.0
            one[0, LANES + base + i] = 1.0
            sel[i * LANES + h, LANES + base + _N_SPLIT + i] = -1.0
    return sel, one


_STATS_ROWS = (tuple(("qa", p) for p in range(HEAD_PAIRS)) + tuple(("ka", p) for p in range(HEAD_PAIRS))
               + ("c_first", "c_last"))
_CONV_ROWS = 16
_SUBLANES = 8


def _inproj_kernel(x_ref, g_ref, w_ref, fb_ref, tri_ref, sel_ref, one_ref, cw_ref, cb_ref, lg_ref, lb_ref,
                   qx_ref, kx_ref, va_ref, cnv_ref, qh_ref, fh_ref, ih_ref, gh_ref, stats_ref,
                   carry_ref, halo_ref, hs_ref):
    tm = x_ref.shape[1]

    @pl.when(pl.program_id(1) == 0)
    def _():
        carry_ref[...] = jnp.zeros_like(carry_ref)
        halo_ref[...] = jnp.zeros_like(halo_ref)

    h = _rmsnorm_rows(x_ref[0], g_ref[...]).astype(BF16)
    lane = lax.broadcasted_iota(jnp.int32, (1, LANES), 1)
    direct = {"va": va_ref, "qh": qh_ref, "fh": fh_ref, "ih": ih_ref, "gh": gh_ref}
    cin, bias = {}, {}
    stats = {}
    same_head = (lax.broadcasted_iota(jnp.int32, (LANES, LANES), 0) // HEAD_DIM
                 == lax.broadcasted_iota(jnp.int32, (LANES, LANES), 1) // HEAD_DIM).astype(BF16)

    def forget_bias(z):
        log_f = jax.nn.log_sigmoid(z + fb_ref[...])
        tri = tri_ref[...]
        parts = _dot(tri, jnp.concatenate(_split_bf16(log_f, _N_SPLIT), axis=1))
        c = carry_ref[...] + sum(parts[:, i * LANES:(i + 1) * LANES] for i in range(_N_SPLIT))
        carry_ref[...] = c[tm - 1:tm, :]
        c2 = c * LOG2E
        stats["c_first"], stats["c_last"] = c2[0:1, :], c2[tm - 1:tm, :]
        terms = jnp.concatenate(_split_bf16(c2, _N_SPLIT), axis=1)
        bias["all"] = _dot(terms, sel_ref[...]) + one_ref[...]

    def conv_input():
        half = D_CONV // LANES
        glu = jnp.concatenate([cin[i] * jax.nn.sigmoid(cin[half + i]) for i in range(half)], axis=1)
        hs_ref[0, 0:CONV_HALO, :] = halo_ref[...]
        hs_ref[0, CONV_HALO:CONV_HALO + tm, :] = glu
        halo_ref[...] = glu[tm - CONV_HALO:, :]
        span = tm + CONV_HALO - _SUBLANES
        for s in range(1, _SUBLANES):
            hs_ref[s, 0:span, :] = hs_ref[0, s:s + span, :]

    def conv_rows(r0, after):
        first = CONV_HALO - (CONV_WIDTH - 1)
        acc = jnp.minimum(jnp.abs(after[0:_CONV_ROWS, 0:D_CONV]), 0.0)
        for tap in range(CONV_WIDTH):
            off = first + tap
            base = r0 + off // _SUBLANES * _SUBLANES
            taps = jnp.tile(cw_ref[tap], (_CONV_ROWS // _SUBLANES, 1))
            acc = acc + hs_ref[off % _SUBLANES, base:base + _CONV_ROWS, :] * taps
        acc = acc + cb_ref[...]
        mu = jnp.mean(acc, axis=-1, keepdims=True)
        xc = acc - mu
        var = jnp.mean(xc * xc, axis=-1, keepdims=True)
        y = xc * lax.rsqrt(var + LN_EPS) * lg_ref[...] + lb_ref[...]
        cnv_ref[0, r0:r0 + _CONV_ROWS, :] = (y * jax.nn.sigmoid(y)).astype(cnv_ref.dtype)

    def head_pair(name, z, pair, packed_bias, o_ref):
        norm2 = _dot((z * z).astype(BF16), same_head)
        stats[(name, pair)] = jnp.max(norm2, axis=0, keepdims=True)
        for e in range(2):
            hd = 2 * pair + e
            blk = pltpu.roll(z, HEAD_DIM, axis=1) if e else z
            own_bias = pltpu.roll(packed_bias, HEAD_DIM - hd * _BIAS_STRIDE, axis=1)
            tail = jnp.where(lane < HEAD_DIM + _BIAS_LANES, own_bias, 0.0)
            o_ref[0, :, hd * LANES:(hd + 1) * LANES] = jnp.where(lane < HEAD_DIM, blk, tail).astype(o_ref.dtype)

    def route(name, i, z):
        if name in direct:
            direct[name][0, :, i * LANES:(i + 1) * LANES] = z.astype(direct[name].dtype)
        elif name == "fa":
            forget_bias(z)
        elif name == "cin":
            cin[i] = z
            if len(cin) == 2 * D_CONV // LANES:
                conv_input()
        elif name == "qa":
            head_pair(name, z * (HEAD_DIM ** -0.5 * LOG2E), i, bias["all"][:, :LANES], qx_ref)
        elif name == "ka":
            head_pair(name, z, i, bias["all"][:, LANES:], kx_ref)

    conv_starts = list(range(0, tm, _CONV_ROWS))
    conv_slots = len(_IN_CHUNK_ORDER) - _IN_CHUNK_ORDER.index(3) - 1
    done = 0
    for pos, chunk in enumerate(_IN_CHUNK_ORDER):
        z = _dot(h, w_ref[:, chunk * MXU_COLS:(chunk + 1) * MXU_COLS])
        for e in range(MXU_COLS // LANES):
            name, i = _IN_TILES[chunk * (MXU_COLS // LANES) + e]
            route(name, i, z[:, e * LANES:(e + 1) * LANES])
        slot = pos - (len(_IN_CHUNK_ORDER) - conv_slots)
        if slot >= 0:
            upto = -(-(slot + 1) * len(conv_starts) // conv_slots)
            for r0 in conv_starts[done:upto]:
                conv_rows(r0, z)
            done = upto
    stats_ref[0, 0] = jnp.concatenate([stats[k] for k in _STATS_ROWS], axis=0)


def _inproj(x, g, w_all, layer, fb, tri, sel, one, cw, cb, ln_g, ln_b):
    b, t, d = x.shape
    tm = tri.shape[0]
    tile = lambda n: pl.BlockSpec((1, tm, n), lambda i, j: (i, j, 0))
    w_spec = pl.BlockSpec((None,) + w_all.shape[1:], lambda i, j: (layer, 0, 0), pipeline_mode=pl.Buffered(1))
    consts = (g, w_all, fb, tri, sel, one, cw, cb, ln_g, ln_b)
    return pl.pallas_call(
        _inproj_kernel,
        grid=(b, t // tm),
        in_specs=[tile(d)] + [w_spec if a is w_all else _resident(a.shape) for a in consts],
        out_specs=[tile(n) for _, n, _ in _IN_OUTPUTS]
        + [pl.BlockSpec((1, 1, _SUBLANES, LANES), lambda i, j: (i, j, 0, 0))],
        out_shape=[jax.ShapeDtypeStruct((b, t, n), dt) for _, n, dt in _IN_OUTPUTS]
        + [jax.ShapeDtypeStruct((b, t // tm, _SUBLANES, LANES), F32)],
        scratch_shapes=[pltpu.VMEM((1, LANES), F32), pltpu.VMEM((CONV_HALO, D_CONV), F32),
                        pltpu.VMEM((_SUBLANES, tm + CONV_HALO, D_CONV), F32)],
        compiler_params=_compiler_params(("parallel", "arbitrary")),
        name="inproj",
    )(x, *consts)


_FIRST, _LAST, _MASKED, _ACTIVE = 1, 2, 4, 8
_EXP2_ZERO = -160.0
_NORM_SLACK = 1.02


def _causal_tables(t, tq, tk):
    ii, jj, fl = [], [], []
    for i in range(t // tq):
        nj = -(-((i + 1) * tq) // tk)
        for j in range(nj):
            flag = (_FIRST if j == 0 else 0) | (_LAST if j == nj - 1 else 0)
            if (j + 1) * tk - 1 > i * tq:
                flag |= _MASKED
            ii.append(i), jj.append(j), fl.append(flag)
    return (np.asarray(ii, np.int32), np.asarray(jj, np.int32), np.asarray(fl, np.int32))


def _pruned_tables(stats, tq, tk, t):
    ii0, jj0, fl0 = _causal_tables(t, tq, tk)
    heads = slice(0, ATT_HEADS)

    def norms(name):
        first = _STATS_ROWS.index((name, 0))
        sq = stats[:, :, first:first + HEAD_PAIRS, ::HEAD_DIM]
        return jnp.sqrt(sq.reshape(sq.shape[0], sq.shape[1], ATT_HEADS)) * _NORM_SLACK

    qn, kn = norms("qa"), norms("ka")
    c_first = stats[:, :, _STATS_ROWS.index("c_first"), heads]
    c_last = stats[:, :, _STATS_ROWS.index("c_last"), heads]
    bound = (qn[:, :, None] * kn[:, None, :] + c_first[:, :, None] - c_last[:, None, :]
             + (qn * kn)[:, :, None])
    keep = jnp.any(bound > _EXP2_ZERO, axis=-1)
    n_tiles, n_steps = keep.shape[1], len(ii0)
    tile = np.arange(n_tiles)
    keep = (keep | (tile[:, None] == tile[None, :])) & (tile[None, :] <= tile[:, None])

    def running_count(flags01):
        n = flags01.shape[-1]
        upto = np.tril(np.ones((n, n), np.int32))
        return jnp.sum(flags01[..., None, :].astype(jnp.int32) * upto, axis=-1)

    j_start = jnp.sum((running_count(keep) == 0).astype(jnp.int32), axis=-1)
    step_tile = (ii0[:, None] == tile[None, :]).astype(np.int32)
    start = jnp.sum(j_start[:, None, :] * step_tile[None], axis=-1)
    active = jj0[None, :] >= start
    flags = jnp.where(active, (fl0 & ~_FIRST)[None, :] | _ACTIVE | jnp.where(jj0[None, :] == start, _FIRST, 0), 0)
    rank = running_count(active) - 1
    slot = np.arange(n_steps)
    pick = active[:, :, None] & (rank[:, :, None] == slot[None, None, :])
    compact = lambda a: jnp.sum(jnp.where(pick, a[..., None], 0), axis=1)
    live = slot[None, :] <= rank[:, -1:]
    ii = jnp.where(live, compact(jnp.asarray(ii0)[None, :]), ii0[-1])
    jj = jnp.where(live, compact(jnp.asarray(jj0)[None, :]), jj0[-1])
    n_live = jnp.max(rank[:, -1]) + 1
    return ii.astype(jnp.int32), jj.astype(jnp.int32), compact(flags).astype(jnp.int32), n_live.astype(jnp.int32)


def _attn_kernel(ii_ref, jj_ref, fl_ref, q_ref, k_ref, v_ref, o_ref, m_ref, acc_ref, *, tq, tk):
    bb, step = pl.program_id(0), pl.program_id(1)
    i, j, flag = ii_ref[bb, step], jj_ref[bb, step], fl_ref[bb, step]
    lane = lax.broadcasted_iota(jnp.int32, (1, LANES), 1)

    @pl.when((flag & _FIRST) != 0)
    def _():
        m_ref[...] = jnp.full_like(m_ref, -jnp.inf)
        acc_ref[...] = jnp.zeros_like(acc_ref)

    def scores(h):
        lanes = slice(h * LANES, (h + 1) * LANES)
        return _nt_dot(q_ref[0, :, lanes], k_ref[0, :, lanes])

    def block(masked):
        if masked:
            row = i * tq + lax.broadcasted_iota(jnp.int32, (tq, tk), 0)
            col = j * tk + lax.broadcasted_iota(jnp.int32, (tq, tk), 1)
            keep = col <= row
        s_next = scores(0)
        for h in range(ATT_HEADS):
            s = s_next
            if h + 1 < ATT_HEADS:
                s_next = scores(h + 1)
            if masked:
                s = jnp.where(keep, s, -jnp.inf)
            pair = slice((h // 2) * LANES, (h // 2 + 1) * LANES)
            own = (lane < HEAD_DIM) if h % 2 == 0 else (lane >= HEAD_DIM)
            v2 = v_ref[0, :, pair]
            v1 = jnp.where(own, v2, jnp.ones_like(v2))
            m_prev = m_ref[h]
            m_new = jnp.maximum(m_prev, jnp.max(s, axis=-1, keepdims=True))
            p = jnp.exp2(s - jnp.tile(m_new, (1, tk // LANES)))
            acc_ref[h] = jnp.exp2(m_prev - m_new) * acc_ref[h] + _dot(p.astype(BF16), v1)
            m_ref[h] = m_new

    @pl.when((flag & _MASKED) != 0)
    def _():
        block(True)

    @pl.when((flag & (_MASKED | _ACTIVE)) == _ACTIVE)
    def _():
        block(False)

    @pl.when((flag & _LAST) != 0)
    def _():
        for p in range(HEAD_PAIRS):
            a0, a1 = acc_ref[2 * p], acc_ref[2 * p + 1]
            o0 = a0 / pltpu.roll(a0, HEAD_DIM, axis=1)
            o1 = a1 / pltpu.roll(a1, HEAD_DIM, axis=1)
            o_ref[0, :, p * LANES:(p + 1) * LANES] = jnp.where(lane < HEAD_DIM, o0, o1).astype(o_ref.dtype)


def _attention(qx, kx, va, stats, tq, tk):
    b, t, _ = va.shape
    ii, jj, fl, n_live = _pruned_tables(stats, tq, tk, t)
    q_map = lambda bb, s, ii, jj, fl: (bb, ii[bb, s], 0)
    k_map = lambda bb, s, ii, jj, fl: (bb, jj[bb, s], 0)
    grid_spec = pltpu.PrefetchScalarGridSpec(
        num_scalar_prefetch=3,
        grid=(b, n_live),
        in_specs=[pl.BlockSpec((1, tq, D_ATT_X), q_map),
                  pl.BlockSpec((1, tk, D_ATT_X), k_map),
                  pl.BlockSpec((1, tk, D_ATT), k_map)],
        out_specs=pl.BlockSpec((1, tq, D_ATT), q_map),
        scratch_shapes=[pltpu.VMEM((ATT_HEADS, tq, LANES), F32), pltpu.VMEM((ATT_HEADS, tq, LANES), F32)],
    )
    return pl.pallas_call(
        functools.partial(_attn_kernel, tq=tq, tk=tk),
        grid_spec=grid_spec,
        out_shape=jax.ShapeDtypeStruct((b, t, D_ATT), BF16),
        compiler_params=_compiler_params(("parallel", "arbitrary")),
        name="fox_attention",
    )(ii, jj, fl, qx, kx, va)


HGRN_CHUNK = 128
_HGRN_LEVELS = HGRN_CHUNK.bit_length()


def _hgrn_level_map():
    t = np.arange(HGRN_CHUNK)[:, None]
    s = np.arange(HGRN_CHUNK)[None, :]
    high_bit = np.floor(np.log2(np.maximum(t ^ s, 1))).astype(np.int32)
    level = np.where(t == s, 0, np.where(s < t, high_bit + 1, -1)).astype(np.int32)
    return np.concatenate([level, level], axis=1)


def _hgrn_kernel(q_ref, f_ref, v_ref, g_ref, lbl_ref, gn_ref, tri_ref, lvl_ref, o_ref, st_ref,
                 *, layer, chunks):
    c = HGRN_CHUNK

    @pl.when(pl.program_id(1) == 0)
    def _():
        st_ref[...] = jnp.zeros_like(st_ref)

    z = lbl_ref[...]
    e = jnp.exp(z - jnp.max(z, axis=0, keepdims=True))
    if layer > 0:
        lb_all = jnp.sum(e[1:layer + 1], axis=0, keepdims=True) / jnp.sum(e, axis=0, keepdims=True)
    else:
        lb_all = jnp.zeros((1, D_HGRN), F32)

    lane = lax.broadcasted_iota(jnp.int32, (1, LANES), 1)
    head0 = lane < HEAD_DIM
    row = lax.broadcasted_iota(jnp.int32, (c, LANES), 0)
    level = lvl_ref[...]
    vi = lax.broadcasted_iota(jnp.int32, (LANES, LANES), 0) // HEAD_DIM
    di = lax.broadcasted_iota(jnp.int32, (LANES, LANES), 1) // HEAD_DIM
    same_head = vi == di

    def head_split(x):
        return jnp.where(head0, x, 0.0), jnp.where(head0, 0.0, x)

    def head_stack(x0, x1):
        return jnp.concatenate([x0, x1], axis=0).astype(BF16)

    def start(rows, p):
        cols = slice(p * LANES, (p + 1) * LANES)
        lb = lb_all[:, cols]
        q = q_ref[0, rows, cols]
        v2 = v_ref[0, rows, cols].astype(F32)
        f = lb + (1.0 - lb) * jax.nn.sigmoid(f_ref[0, rows, cols])
        kk = 1.0 - f
        g = jnp.log(f) * LOG2E
        b2 = _dot(tri_ref[...], jnp.concatenate(_split_bf16(g, 2), axis=1))
        b_incl = b2[:, :LANES] + b2[:, LANES:]
        kk0, kk1 = head_split(kk)
        return dict(cols=cols, p=p, q=q, v2=v2, kk=kk, kk0=kk0, kk1=kk1, g=g, b_incl=b_incl,
                    kk_stack=head_stack(kk0, kk1),
                    a=jnp.zeros((c, 2 * c), F32), seg_end=b_incl)

    def level_step(s, lvl):
        q, kk, b_incl = s["q"], s["kk"], s["b_incl"]
        m = 1 << max(lvl - 1, 0)
        if lvl == 0:
            qt, kt = q, s["kk_stack"]
        elif lvl == 1:
            qt, kt = q * jnp.exp2(s["g"]), s["kk_stack"]
        else:
            half = m // 2
            seg_end = s["seg_end"]
            seg_end = jnp.where((row & half) == 0, pltpu.roll(seg_end, c - half, axis=0), seg_end)
            prev_end = jnp.where(row >= m, pltpu.roll(seg_end, m, axis=0), 0.0)
            qt = q * jnp.exp2(b_incl - prev_end)
            later = jnp.exp2(seg_end - b_incl)
            kt = head_stack(s["kk0"] * later, s["kk1"] * later)
            s["seg_end"] = seg_end
        s["a"] = jnp.where(level == lvl, _nt_dot(qt.astype(BF16), kt), s["a"])

    def finish(s, rows):
        q, kk, v2, b_incl, p, cols = s["q"], s["kk"], s["v2"], s["b_incl"], s["p"], s["cols"]
        o = _dot(s["a"].astype(BF16), head_stack(*head_split(v2)))
        st = st_ref[p]
        o = o + _nt_dot((q * jnp.exp2(b_incl)).astype(BF16), st.astype(BF16))
        b_end = b_incl[c - 1:c, :]
        upd = _dot(v2.T.astype(BF16), (kk * jnp.exp2(b_end - b_incl)).astype(BF16))
        st_ref[p] = st * jnp.exp2(b_end) + jnp.where(same_head, upd, 0.0)

        o2 = o * o
        ms0 = jnp.sum(jnp.where(head0, o2, 0.0), axis=-1, keepdims=True) * (1.0 / HEAD_DIM)
        ms1 = jnp.sum(jnp.where(head0, 0.0, o2), axis=-1, keepdims=True) * (1.0 / HEAD_DIM)
        inv = jnp.where(head0, lax.rsqrt(ms0 + RMS_EPS), lax.rsqrt(ms1 + RMS_EPS))
        gate = g_ref[0, rows, cols]
        o_ref[0, rows, cols] = (o * inv * gn_ref[...] * (gate * jax.nn.sigmoid(gate))).astype(o_ref.dtype)

    for ch in range(chunks):
        rows = slice(ch * c, (ch + 1) * c)
        states = [start(rows, p) for p in range(HEAD_PAIRS)]
        for lvl in range(_HGRN_LEVELS):
            for s in states:
                level_step(s, lvl)
        for s in states:
            finish(s, rows)


def _hgrn(qh, fh, ih, gh, lb_logits, gn, tri, lvl, layer, tt):
    b, t, _ = qh.shape
    tile = pl.BlockSpec((1, tt, D_HGRN), lambda i, j: (i, j, 0))
    return pl.pallas_call(
        functools.partial(_hgrn_kernel, layer=layer, chunks=tt // HGRN_CHUNK),
        grid=(b, t // tt),
        in_specs=[tile, tile, tile, tile, _resident(lb_logits.shape),
                  _resident(gn.shape), _resident(tri.shape), _resident(lvl.shape)],
        out_specs=tile,
        out_shape=jax.ShapeDtypeStruct((b, t, D_HGRN), BF16),
        scratch_shapes=[pltpu.VMEM((HEAD_PAIRS, LANES, LANES), F32)],
        compiler_params=_compiler_params(("parallel", "arbitrary")),
        name="hgrn2",
    )(qh, fh, ih, gh, lb_logits, gn, tri, lvl)


_FFN_COLS = 256


def _ffn_kernel(x_ref, att_ref, cnv_ref, hg_ref, wo_ref, g_ref, wi_ref, w2_ref, gf_ref, o_ref, a_ref,
                *, d_ff, final):
    mix = jnp.concatenate([att_ref[...], cnv_ref[...], hg_ref[...]], axis=1)
    x1 = x_ref[...] + _dot(mix, wo_ref[...])
    h = _rmsnorm_rows(x1, g_ref[...]).astype(BF16)
    for c0 in range(0, d_ff, _FFN_COLS):
        gate = _dot(h, wi_ref[:, c0:c0 + _FFN_COLS])
        up = _dot(h, wi_ref[:, d_ff + c0:d_ff + c0 + _FFN_COLS])
        a_ref[:, c0:c0 + _FFN_COLS] = (gate * jax.nn.sigmoid(gate) * up).astype(BF16)
    y = x1 + _dot(a_ref[...], w2_ref[...])
    if final:
        y = _rmsnorm_rows(y, gf_ref[...])
    o_ref[...] = y


def _out_ffn(x2d, att, cnv, hg, wo, g, wi, w2, gf, tm, final):
    m, d = x2d.shape
    d_ff = w2.shape[0]
    row = lambda n: pl.BlockSpec((tm, n), lambda i: (i, 0))
    return pl.pallas_call(
        functools.partial(_ffn_kernel, d_ff=d_ff, final=final),
        grid=(m // tm,),
        in_specs=[row(d), row(D_ATT), row(D_CONV), row(D_HGRN), _resident(wo.shape), _resident(g.shape),
                  _resident(wi.shape), _resident(w2.shape), _resident(gf.shape)],
        out_specs=row(d),
        out_shape=jax.ShapeDtypeStruct((m, d), F32),
        scratch_shapes=[pltpu.VMEM((tm, d_ff), BF16)],
        compiler_params=_compiler_params(("parallel",)),
        name="out_ffn",
    )(x2d, att, cnv, hg, wo, g, wi, w2, gf)


def _tiles(t):
    return dict(tm=min(512, t), tq=min(512, t), tk=min(512, t), tt=min(256, t))


_W_IN_SIZES = (D_ATT, D_ATT, D_ATT, ATT_HEADS, 2 * D_CONV, D_HGRN, D_HGRN, D_HGRN, D_HGRN)
_W_IN_AFTER_F = 3 * D_ATT + ATT_HEADS
_W_IN_TAIL = sum(_W_IN_SIZES[4:])


def _w_in_layout_kernel(w_ref, o_ref):
    x = w_ref[0]
    lane = lax.broadcasted_iota(jnp.int32, (1, LANES), 1)
    zero = jnp.zeros((x.shape[0], LANES), F32)

    f_tile = jnp.where(lane < ATT_HEADS, x[:, 3 * D_ATT:3 * D_ATT + LANES], zero)
    shift = _W_IN_AFTER_F % LANES
    tail0 = _W_IN_AFTER_F - shift
    tail = pltpu.roll(x, x.shape[1] - shift, axis=1)[:, tail0:tail0 + _W_IN_TAIL]
    o_ref[0] = jnp.concatenate([x[:, 2 * D_ATT:3 * D_ATT], f_tile, tail, x[:, :2 * D_ATT]], axis=1).astype(o_ref.dtype)


def _w_in_layout(w_in):
    depth, d, d_in = w_in.shape
    rows = min(256, d)
    width = -(-d_in // LANES) * LANES
    d_out = sum(n for _, n in _IN_COLS)
    return pl.pallas_call(
        _w_in_layout_kernel,
        grid=(depth, d // rows),
        in_specs=[pl.BlockSpec((1, rows, width), lambda l, i: (l, i, 0))],
        out_specs=pl.BlockSpec((1, rows, d_out), lambda l, i: (l, i, 0)),
        out_shape=jax.ShapeDtypeStruct((depth, d, d_out), BF16),
        compiler_params=_compiler_params(("parallel", "parallel")),
        name="w_in_layout",
    )(w_in)


def kernel(x, norm_mix_g, w_in, fgate_b, conv_w, conv_b, conv_ln_g, conv_ln_b, hgrn_lb_logits, hgrn_norm_g,
           w_out, norm_ffn_g, w_ffn_in, w_ffn_out, norm_final_g):
    b, t, d = x.shape
    depth = w_in.shape[0]
    tl = _tiles(t)
    row = lambda a: a.reshape(1, -1).astype(F32)

    tri = jnp.asarray(np.tril(np.ones((tl["tm"], tl["tm"]), np.float32)), BF16)
    sel, one = _bias_lane_constants()
    sel, one = jnp.asarray(sel, BF16), jnp.asarray(one)
    tri_h = jnp.asarray(np.tril(np.ones((HGRN_CHUNK, HGRN_CHUNK), np.float32)), BF16)
    lvl = jnp.asarray(_hgrn_level_map())
    lb_logits = hgrn_lb_logits.astype(F32)
    gf = row(norm_final_g)

    w_all = _w_in_layout(w_in)
    for l in range(depth):
        fb = row(jnp.pad(fgate_b[l], (0, LANES - ATT_HEADS)))
        cw = jnp.broadcast_to(conv_w[l][:, None, :], (CONV_WIDTH, _SUBLANES, D_CONV))
        qx, kx, va, cnv, qh, fh, ih, gh, stats = _inproj(x, row(norm_mix_g[l]), w_all, l, fb, tri, sel, one, cw,
                                                  row(conv_b[l]), row(conv_ln_g[l]), row(conv_ln_b[l]))

        att = _attention(qx, kx, va, stats, tl["tq"], tl["tk"])

        gn = row(jnp.tile(hgrn_norm_g[l], LANES // HEAD_DIM))
        hg = _hgrn(qh, fh, ih, gh, lb_logits, gn, tri_h, lvl, l, tl["tt"])

        x = _out_ffn(x.reshape(b * t, d), att.reshape(b * t, -1), cnv.reshape(b * t, -1), hg.reshape(b * t, -1),
                     w_out[l].astype(BF16), row(norm_ffn_g[l]), w_ffn_in[l].astype(BF16),
                     w_ffn_out[l].astype(BF16), gf, tl["tm"], final=(l == depth - 1)).reshape(b, t, d)
    return x
```

```python
import functools

import numpy as np
import jax
import jax.numpy as jnp
from jax import lax
from jax.experimental import pallas as pl
from jax.experimental.pallas import tpu as pltpu

F32 = jnp.float32
BF16 = jnp.bfloat16

HEAD_DIM = 64
ATT_HEADS = 6
D_ATT = ATT_HEADS * HEAD_DIM
D_CONV = 256
CONV_WIDTH = 31
HGRN_HEADS = 6
D_HGRN = HGRN_HEADS * HEAD_DIM
RMS_EPS = 1e-6
LN_EPS = 1e-5

LANES = 128
HEAD_PAIRS = D_ATT // LANES
CONV_HALO = 32
VMEM_LIMIT_BYTES = 56 * 1024 * 1024

D_ATT_X = ATT_HEADS * LANES
LOG2E = 1.4426950408889634

_IN_COLS = (("va", D_ATT), ("fa", LANES), ("cin", 2 * D_CONV), ("qh", D_HGRN), ("fh", D_HGRN),
            ("ih", D_HGRN), ("gh", D_HGRN), ("qa", D_ATT), ("ka", D_ATT))
_IN_TILES = tuple((name, i) for name, n in _IN_COLS for i in range(n // LANES))
MXU_COLS = 2 * LANES
_IN_CHUNK_ORDER = (2, 3, 1, 0) + tuple(range(4, len(_IN_TILES) * LANES // MXU_COLS))
_IN_OUTPUTS = (("qx", D_ATT_X, BF16), ("kx", D_ATT_X, BF16), ("va", D_ATT, BF16), ("cnv", D_CONV, BF16),
               ("qh", D_HGRN, F32), ("fh", D_HGRN, F32), ("ih", D_HGRN, BF16), ("gh", D_HGRN, F32))
_N_SPLIT = 3


def _nt_dot(a, b):
    return lax.dot_general(a, b, (((1,), (1,)), ((), ())), preferred_element_type=F32)


def _dot(a, b):
    return jnp.dot(a, b, preferred_element_type=F32)


def _rmsnorm_rows(x, g):
    return x * lax.rsqrt(jnp.mean(x * x, axis=-1, keepdims=True) + RMS_EPS) * g


def _compiler_params(semantics):
    return pltpu.CompilerParams(dimension_semantics=semantics, vmem_limit_bytes=VMEM_LIMIT_BYTES)


def _resident(shape):
    nd = len(shape)
    return pl.BlockSpec(shape, lambda *_: (0,) * nd, pipeline_mode=pl.Buffered(1))


def _split_bf16(x, n):
    terms = []
    for _ in range(n - 1):
        t = x.astype(BF16)
        terms.append(t)
        x = x - t.astype(F32)
    terms.append(x.astype(BF16))
    return terms


_BIAS_LANES = 2 * _N_SPLIT
_BIAS_STRIDE = 8


def _bias_lane_constants():
    sel = np.zeros((_N_SPLIT * LANES, 2 * LANES), np.float32)
    one = np.zeros((1, 2 * LANES), np.float32)
    for h in range(ATT_HEADS):
        base = h * _BIAS_STRIDE
        for i in range(_N_SPLIT):
            sel[i * LANES + h, base + i] = 1.0
            one[0, base + _N_SPLIT + i] = 1.0
            one[0, LANES + base + i] = 1.0
            sel[i * LANES + h, LANES + base + _N_SPLIT + i] = -1.0
    return sel, one


_STATS_ROWS = (tuple(("qa", p) for p in range(HEAD_PAIRS)) + tuple(("ka", p) for p in range(HEAD_PAIRS))
               + ("c_first", "c_last"))
_CONV_ROWS = 16
_SUBLANES = 8


def _inproj_kernel(x_ref, g_ref, w_ref, fb_ref, tri_ref, sel_ref, one_ref, cw_ref, cb_ref, lg_ref, lb_ref,
                   qx_ref, kx_ref, va_ref, cnv_ref, qh_ref, fh_ref, ih_ref, gh_ref, stats_ref,
                   carry_ref, halo_ref, hs_ref):
    tm = x_ref.shape[1]

    @pl.when(pl.program_id(1) == 0)
    def _():
        carry_ref[...] = jnp.zeros_like(carry_ref)
        halo_ref[...] = jnp.zeros_like(halo_ref)

    h = _rmsnorm_rows(x_ref[0], g_ref[...]).astype(BF16)
    lane = lax.broadcasted_iota(jnp.int32, (1, LANES), 1)
    direct = {"va": va_ref, "qh": qh_ref, "fh": fh_ref, "ih": ih_ref, "gh": gh_ref}
    cin, bias = {}, {}
    stats = {}
    same_head = (lax.broadcasted_iota(jnp.int32, (LANES, LANES), 0) // HEAD_DIM
                 == lax.broadcasted_iota(jnp.int32, (LANES, LANES), 1) // HEAD_DIM).astype(BF16)

    def forget_bias(z):
        log_f = jax.nn.log_sigmoid(z + fb_ref[...])
        tri = tri_ref[...]
        parts = _dot(tri, jnp.concatenate(_split_bf16(log_f, _N_SPLIT), axis=1))
        c = carry_ref[...] + sum(parts[:, i * LANES:(i + 1) * LANES] for i in range(_N_SPLIT))
        carry_ref[...] = c[tm - 1:tm, :]
        c2 = c * LOG2E
        stats["c_first"], stats["c_last"] = c2[0:1, :], c2[tm - 1:tm, :]
        terms = jnp.concatenate(_split_bf16(c2, _N_SPLIT), axis=1)
        bias["all"] = _dot(terms, sel_ref[...]) + one_ref[...]

    def conv_input():
        half = D_CONV // LANES
        glu = jnp.concatenate([cin[i] * jax.nn.sigmoid(cin[half + i]) for i in range(half)], axis=1)
        hs_ref[0, 0:CONV_HALO, :] = halo_ref[...]
        hs_ref[0, CONV_HALO:CONV_HALO + tm, :] = glu
        halo_ref[...] = glu[tm - CONV_HALO:, :]
        span = tm + CONV_HALO - _SUBLANES
        for s in range(1, _SUBLANES):
            hs_ref[s, 0:span, :] = hs_ref[0, s:s + span, :]

    def conv_rows(r0, after):
        first = CONV_HALO - (CONV_WIDTH - 1)
        acc = jnp.minimum(jnp.abs(after[0:_CONV_ROWS, 0:D_CONV]), 0.0)
        for tap in range(CONV_WIDTH):
            off = first + tap
            base = r0 + off // _SUBLANES * _SUBLANES
            taps = jnp.tile(cw_ref[tap], (_CONV_ROWS // _SUBLANES, 1))
            acc = acc + hs_ref[off % _SUBLANES, base:base + _CONV_ROWS, :] * taps
        acc = acc + cb_ref[...]
        mu = jnp.mean(acc, axis=-1, keepdims=True)
        xc = acc - mu
        var = jnp.mean(xc * xc, axis=-1, keepdims=True)
        y = xc * lax.rsqrt(var + LN_EPS) * lg_ref[...] + lb_ref[...]
        cnv_ref[0, r0:r0 + _CONV_ROWS, :] = (y * jax.nn.sigmoid(y)).astype(cnv_ref.dtype)

    def head_pair(name, z, pair, packed_bias, o_ref):
        norm2 = _dot((z * z).astype(BF16), same_head)
        stats[(name, pair)] = jnp.max(norm2, axis=0, keepdims=True)
        for e in range(2):
            hd = 2 * pair + e
            blk = pltpu.roll(z, HEAD_DIM, axis=1) if e else z
            own_bias = pltpu.roll(packed_bias, HEAD_DIM - hd * _BIAS_STRIDE, axis=1)
            tail = jnp.where(lane < HEAD_DIM + _BIAS_LANES, own_bias, 0.0)
            o_ref[0, :, hd * LANES:(hd + 1) * LANES] = jnp.where(lane < HEAD_DIM, blk, tail).astype(o_ref.dtype)

    def route(name, i, z):
        if name in direct:
            direct[name][0, :, i * LANES:(i + 1) * LANES] = z.astype(direct[name].dtype)
        elif name == "fa":
            forget_bias(z)
        elif name == "cin":
            cin[i] = z
            if len(cin) == 2 * D_CONV // LANES:
                conv_input()
        elif name == "qa":
            head_pair(name, z * (HEAD_DIM ** -0.5 * LOG2E), i, bias["all"][:, :LANES], qx_ref)
        elif name == "ka":
            head_pair(name, z, i, bias["all"][:, LANES:], kx_ref)

    conv_starts = list(range(0, tm, _CONV_ROWS))
    conv_slots = len(_IN_CHUNK_ORDER) - _IN_CHUNK_ORDER.index(3) - 1
    done = 0
    for pos, chunk in enumerate(_IN_CHUNK_ORDER):
        z = _dot(h, w_ref[:, chunk * MXU_COLS:(chunk + 1) * MXU_COLS])
        for e in range(MXU_COLS // LANES):
            name, i = _IN_TILES[chunk * (MXU_COLS // LANES) + e]
            route(name, i, z[:, e * LANES:(e + 1) * LANES])
        slot = pos - (len(_IN_CHUNK_ORDER) - conv_slots)
        if slot >= 0:
            upto = -(-(slot + 1) * len(conv_starts) // conv_slots)
            for r0 in conv_starts[done:upto]:
                conv_rows(r0, z)
            done = upto
    stats_ref[0, 0] = jnp.concatenate([stats[k] for k in _STATS_ROWS], axis=0)


def _inproj(x, g, w_all, layer, fb, tri, sel, one, cw, cb, ln_g, ln_b):
    b, t, d = x.shape
    tm = tri.shape[0]
    tile = lambda n: pl.BlockSpec((1, tm, n), lambda i, j: (i, j, 0))
    w_spec = pl.BlockSpec((None,) + w_all.shape[1:], lambda i, j: (layer, 0, 0), pipeline_mode=pl.Buffered(1))
    consts = (g, w_all, fb, tri, sel, one, cw, cb, ln_g, ln_b)
    return pl.pallas_call(
        _inproj_kernel,
        grid=(b, t // tm),
        in_specs=[tile(d)] + [w_spec if a is w_all else _resident(a.shape) for a in consts],
        out_specs=[tile(n) for _, n, _ in _IN_OUTPUTS]
        + [pl.BlockSpec((1, 1, _SUBLANES, LANES), lambda i, j: (i, j, 0, 0))],
        out_shape=[jax.ShapeDtypeStruct((b, t, n), dt) for _, n, dt in _IN_OUTPUTS]
        + [jax.ShapeDtypeStruct((b, t // tm, _SUBLANES, LANES), F32)],
        scratch_shapes=[pltpu.VMEM((1, LANES), F32), pltpu.VMEM((CONV_HALO, D_CONV), F32),
                        pltpu.VMEM((_SUBLANES, tm + CONV_HALO, D_CONV), F32)],
        compiler_params=_compiler_params(("parallel", "arbitrary")),
        name="inproj",
    )(x, *consts)


_FIRST, _LAST, _MASKED, _ACTIVE = 1, 2, 4, 8
_EXP2_ZERO = -160.0
_NORM_SLACK = 1.02


def _causal_tables(t, tq, tk):
    ii, jj, fl = [], [], []
    for i in range(t // tq):
        nj = -(-((i + 1) * tq) // tk)
        for j in range(nj):
            flag = (_FIRST if j == 0 else 0) | (_LAST if j == nj - 1 else 0)
            if (j + 1) * tk - 1 > i * tq:
                flag |= _MASKED
            ii.append(i), jj.append(j), fl.append(flag)
    return (np.asarray(ii, np.int32), np.asarray(jj, np.int32), np.asarray(fl, np.int32))


def _pruned_tables(stats, tq, tk, t):
    ii0, jj0, fl0 = _causal_tables(t, tq, tk)
    heads = slice(0, ATT_HEADS)

    def norms(name):
        first = _STATS_ROWS.index((name, 0))
        sq = stats[:, :, first:first + HEAD_PAIRS, ::HEAD_DIM]
        return jnp.sqrt(sq.reshape(sq.shape[0], sq.shape[1], ATT_HEADS)) * _NORM_SLACK

    qn, kn = norms("qa"), norms("ka")
    c_first = stats[:, :, _STATS_ROWS.index("c_first"), heads]
    c_last = stats[:, :, _STATS_ROWS.index("c_last"), heads]
    bound = (qn[:, :, None] * kn[:, None, :] + c_first[:, :, None] - c_last[:, None, :]
             + (qn * kn)[:, :, None])
    keep = jnp.any(bound > _EXP2_ZERO, axis=-1)
    n_tiles, n_steps = keep.shape[1], len(ii0)
    tile = np.arange(n_tiles)
    keep = (keep | (tile[:, None] == tile[None, :])) & (tile[None, :] <= tile[:, None])

    def running_count(flags01):
        n = flags01.shape[-1]
        upto = np.tril(np.ones((n, n), np.int32))
        return jnp.sum(flags01[..., None, :].astype(jnp.int32) * upto, axis=-1)

    j_start = jnp.sum((running_count(keep) == 0).astype(jnp.int32), axis=-1)
    step_tile = (ii0[:, None] == tile[None, :]).astype(np.int32)
    start = jnp.sum(j_start[:, None, :] * step_tile[None], axis=-1)
    active = jj0[None, :] >= start
    flags = jnp.where(active, (fl0 & ~_FIRST)[None, :] | _ACTIVE | jnp.where(jj0[None, :] == start, _FIRST, 0), 0)
    rank = running_count(active) - 1
    slot = np.arange(n_steps)
    pick = active[:, :, None] & (rank[:, :, None] == slot[None, None, :])
    compact = lambda a: jnp.sum(jnp.where(pick, a[..., None], 0), axis=1)
    live = slot[None, :] <= rank[:, -1:]
    ii = jnp.where(live, compact(jnp.asarray(ii0)[None, :]), ii0[-1])
    jj = jnp.where(live, compact(jnp.asarray(jj0)[None, :]), jj0[-1])
    n_live = jnp.max(rank[:, -1]) + 1
    return ii.astype(jnp.int32), jj.astype(jnp.int32), compact(flags).astype(jnp.int32), n_live.astype(jnp.int32)


def _attn_kernel(ii_ref, jj_ref, fl_ref, q_ref, k_ref, v_ref, o_ref, m_ref, acc_ref, *, tq, tk):
    bb, step = pl.program_id(0), pl.program_id(1)
    i, j, flag = ii_ref[bb, step], jj_ref[bb, step], fl_ref[bb, step]
    lane = lax.broadcasted_iota(jnp.int32, (1, LANES), 1)

    @pl.when((flag & _FIRST) != 0)
    def _():
        m_ref[...] = jnp.full_like(m_ref, -jnp.inf)
        acc_ref[...] = jnp.zeros_like(acc_ref)

    def scores(h):
        lanes = slice(h * LANES, (h + 1) * LANES)
        return _nt_dot(q_ref[0, :, lanes], k_ref[0, :, lanes])

    def block(masked):
        if masked:
            row = i * tq + lax.broadcasted_iota(jnp.int32, (tq, tk), 0)
            col = j * tk + lax.broadcasted_iota(jnp.int32, (tq, tk), 1)
            keep = col <= row
        s_next = scores(0)
        for h in range(ATT_HEADS):
            s = s_next
            if h + 1 < ATT_HEADS:
                s_next = scores(h + 1)
            if masked:
                s = jnp.where(keep, s, -jnp.inf)
            pair = slice((h // 2) * LANES, (h // 2 + 1) * LANES)
            own = (lane < HEAD_DIM) if h % 2 == 0 else (lane >= HEAD_DIM)
            v2 = v_ref[0, :, pair]
            v1 = jnp.where(own, v2, jnp.ones_like(v2))
            m_prev = m_ref[h]
            m_new = jnp.maximum(m_prev, jnp.max(s, axis=-1, keepdims=True))
            p = jnp.exp2(s - jnp.tile(m_new, (1, tk // LANES)))
            acc_ref[h] = jnp.exp2(m_prev - m_new) * acc_ref[h] + _dot(p.astype(BF16), v1)
            m_ref[h] = m_new

    @pl.when((flag & _MASKED) != 0)
    def _():
        block(True)

    @pl.when((flag & (_MASKED | _ACTIVE)) == _ACTIVE)
    def _():
        block(False)

    @pl.when((flag & _LAST) != 0)
    def _():
        for p in range(HEAD_PAIRS):
            a0, a1 = acc_ref[2 * p], acc_ref[2 * p + 1]
            o0 = a0 / pltpu.roll(a0, HEAD_DIM, axis=1)
            o1 = a1 / pltpu.roll(a1, HEAD_DIM, axis=1)
            o_ref[0, :, p * LANES:(p + 1) * LANES] = jnp.where(lane < HEAD_DIM, o0, o1).astype(o_ref.dtype)


def _attention(qx, kx, va, stats, tq, tk):
    b, t, _ = va.shape
    ii, jj, fl, n_live = _pruned_tables(stats, tq, tk, t)
    q_map = lambda bb, s, ii, jj, fl: (bb, ii[bb, s], 0)
    k_map = lambda bb, s, ii, jj, fl: (bb, jj[bb, s], 0)
    grid_spec = pltpu.PrefetchScalarGridSpec(
        num_scalar_prefetch=3,
        grid=(b, ii.shape[1]),
        in_specs=[pl.BlockSpec((1, tq, D_ATT_X), q_map),
                  pl.BlockSpec((1, tk, D_ATT_X), k_map),
                  pl.BlockSpec((1, tk, D_ATT), k_map)],
        out_specs=pl.BlockSpec((1, tq, D_ATT), q_map),
        scratch_shapes=[pltpu.VMEM((ATT_HEADS, tq, LANES), F32), pltpu.VMEM((ATT_HEADS, tq, LANES), F32)],
    )
    return pl.pallas_call(
        functools.partial(_attn_kernel, tq=tq, tk=tk),
        grid_spec=grid_spec,
        out_shape=jax.ShapeDtypeStruct((b, t, D_ATT), BF16),
        compiler_params=_compiler_params(("parallel", "arbitrary")),
        name="fox_attention",
    )(ii, jj, fl, qx, kx, va)


HGRN_CHUNK = 128
_HGRN_LEVELS = HGRN_CHUNK.bit_length()


def _hgrn_level_map():
    t = np.arange(HGRN_CHUNK)[:, None]
    s = np.arange(HGRN_CHUNK)[None, :]
    high_bit = np.floor(np.log2(np.maximum(t ^ s, 1))).astype(np.int32)
    level = np.where(t == s, 0, np.where(s < t, high_bit + 1, -1)).astype(np.int32)
    return np.concatenate([level, level], axis=1)


def _hgrn_kernel(q_ref, f_ref, v_ref, g_ref, lbl_ref, gn_ref, tri_ref, lvl_ref, o_ref, st_ref,
                 *, layer, chunks):
    c = HGRN_CHUNK

    @pl.when(pl.program_id(1) == 0)
    def _():
        st_ref[...] = jnp.zeros_like(st_ref)

    z = lbl_ref[...]
    e = jnp.exp(z - jnp.max(z, axis=0, keepdims=True))
    if layer > 0:
        lb_all = jnp.sum(e[1:layer + 1], axis=0, keepdims=True) / jnp.sum(e, axis=0, keepdims=True)
    else:
        lb_all = jnp.zeros((1, D_HGRN), F32)

    lane = lax.broadcasted_iota(jnp.int32, (1, LANES), 1)
    head0 = lane < HEAD_DIM
    row = lax.broadcasted_iota(jnp.int32, (c, LANES), 0)
    level = lvl_ref[...]
    vi = lax.broadcasted_iota(jnp.int32, (LANES, LANES), 0) // HEAD_DIM
    di = lax.broadcasted_iota(jnp.int32, (LANES, LANES), 1) // HEAD_DIM
    same_head = vi == di

    def head_split(x):
        return jnp.where(head0, x, 0.0), jnp.where(head0, 0.0, x)

    def head_stack(x0, x1):
        return jnp.concatenate([x0, x1], axis=0).astype(BF16)

    def start(rows, p):
        cols = slice(p * LANES, (p + 1) * LANES)
        lb = lb_all[:, cols]
        q = q_ref[0, rows, cols]
        v2 = v_ref[0, rows, cols].astype(F32)
        f = lb + (1.0 - lb) * jax.nn.sigmoid(f_ref[0, rows, cols])
        kk = 1.0 - f
        g = jnp.log(f) * LOG2E
        b2 = _dot(tri_ref[...], jnp.concatenate(_split_bf16(g, 2), axis=1))
        b_incl = b2[:, :LANES] + b2[:, LANES:]
        kk0, kk1 = head_split(kk)
        return dict(cols=cols, p=p, q=q, v2=v2, kk=kk, kk0=kk0, kk1=kk1, g=g, b_incl=b_incl,
                    kk_stack=head_stack(kk0, kk1),
                    a=jnp.zeros((c, 2 * c), F32), seg_end=b_incl)

    def level_step(s, lvl):
        q, kk, b_incl = s["q"], s["kk"], s["b_incl"]
        m = 1 << max(lvl - 1, 0)
        if lvl == 0:
            qt, kt = q, s["kk_stack"]
        elif lvl == 1:
            qt, kt = q * jnp.exp2(s["g"]), s["kk_stack"]
        else:
            half = m // 2
            seg_end = s["seg_end"]
            seg_end = jnp.where((row & half) == 0, pltpu.roll(seg_end, c - half, axis=0), seg_end)
            prev_end = jnp.where(row >= m, pltpu.roll(seg_end, m, axis=0), 0.0)
            qt = q * jnp.exp2(b_incl - prev_end)
            later = jnp.exp2(seg_end - b_incl)
            kt = head_stack(s["kk0"] * later, s["kk1"] * later)
            s["seg_end"] = seg_end
        s["a"] = jnp.where(level == lvl, _nt_dot(qt.astype(BF16), kt), s["a"])

    def finish(s, rows):
        q, kk, v2, b_incl, p, cols = s["q"], s["kk"], s["v2"], s["b_incl"], s["p"], s["cols"]
        o = _dot(s["a"].astype(BF16), head_stack(*head_split(v2)))
        st = st_ref[p]
        o = o + _nt_dot((q * jnp.exp2(b_incl)).astype(BF16), st.astype(BF16))
        b_end = b_incl[c - 1:c, :]
        upd = _dot(v2.T.astype(BF16), (kk * jnp.exp2(b_end - b_incl)).astype(BF16))
        st_ref[p] = st * jnp.exp2(b_end) + jnp.where(same_head, upd, 0.0)

        o2 = o * o
        ms0 = jnp.sum(jnp.where(head0, o2, 0.0), axis=-1, keepdims=True) * (1.0 / HEAD_DIM)
        ms1 = jnp.sum(jnp.where(head0, 0.0, o2), axis=-1, keepdims=True) * (1.0 / HEAD_DIM)
        inv = jnp.where(head0, lax.rsqrt(ms0 + RMS_EPS), lax.rsqrt(ms1 + RMS_EPS))
        gate = g_ref[0, rows, cols]
        o_ref[0, rows, cols] = (o * inv * gn_ref[...] * (gate * jax.nn.sigmoid(gate))).astype(o_ref.dtype)

    for ch in range(chunks):
        rows = slice(ch * c, (ch + 1) * c)
        states = [start(rows, p) for p in range(HEAD_PAIRS)]
        for lvl in range(_HGRN_LEVELS):
            for s in states:
                level_step(s, lvl)
        for s in states:
            finish(s, rows)


def _hgrn(qh, fh, ih, gh, lb_logits, gn, tri, lvl, layer, tt):
    b, t, _ = qh.shape
    tile = pl.BlockSpec((1, tt, D_HGRN), lambda i, j: (i, j, 0))
    return pl.pallas_call(
        functools.partial(_hgrn_kernel, layer=layer, chunks=tt // HGRN_CHUNK),
        grid=(b, t // tt),
        in_specs=[tile, tile, tile, tile, _resident(lb_logits.shape),
                  _resident(gn.shape), _resident(tri.shape), _resident(lvl.shape)],
        out_specs=tile,
        out_shape=jax.ShapeDtypeStruct((b, t, D_HGRN), BF16),
        scratch_shapes=[pltpu.VMEM((HEAD_PAIRS, LANES, LANES), F32)],
        compiler_params=_compiler_params(("parallel", "arbitrary")),
        name="hgrn2",
    )(qh, fh, ih, gh, lb_logits, gn, tri, lvl)


_FFN_COLS = 256


def _ffn_kernel(x_ref, att_ref, cnv_ref, hg_ref, wo_ref, g_ref, wi_ref, w2_ref, gf_ref, o_ref, a_ref,
                *, d_ff, final):
    mix = jnp.concatenate([att_ref[...], cnv_ref[...], hg_ref[...]], axis=1)
    x1 = x_ref[...] + _dot(mix, wo_ref[...])
    h = _rmsnorm_rows(x1, g_ref[...]).astype(BF16)
    for c0 in range(0, d_ff, _FFN_COLS):
        gate = _dot(h, wi_ref[:, c0:c0 + _FFN_COLS])
        up = _dot(h, wi_ref[:, d_ff + c0:d_ff + c0 + _FFN_COLS])
        a_ref[:, c0:c0 + _FFN_COLS] = (gate * jax.nn.sigmoid(gate) * up).astype(BF16)
    y = x1 + _dot(a_ref[...], w2_ref[...])
    if final:
        y = _rmsnorm_rows(y, gf_ref[...])
    o_ref[...] = y


def _out_ffn(x2d, att, cnv, hg, wo, g, wi, w2, gf, tm, final):
    m, d = x2d.shape
    d_ff = w2.shape[0]
    row = lambda n: pl.BlockSpec((tm, n), lambda i: (i, 0))
    return pl.pallas_call(
        functools.partial(_ffn_kernel, d_ff=d_ff, final=final),
        grid=(m // tm,),
        in_specs=[row(d), row(D_ATT), row(D_CONV), row(D_HGRN), _resident(wo.shape), _resident(g.shape),
                  _resident(wi.shape), _resident(w2.shape), _resident(gf.shape)],
        out_specs=row(d),
        out_shape=jax.ShapeDtypeStruct((m, d), F32),
        scratch_shapes=[pltpu.VMEM((tm, d_ff), BF16)],
        compiler_params=_compiler_params(("parallel",)),
        name="out_ffn",
    )(x2d, att, cnv, hg, wo, g, wi, w2, gf)


def _tiles(t):
    return dict(tm=min(512, t), tq=min(512, t), tk=min(512, t), tt=min(256, t))


_W_IN_SIZES = (D_ATT, D_ATT, D_ATT, ATT_HEADS, 2 * D_CONV, D_HGRN, D_HGRN, D_HGRN, D_HGRN)
_W_IN_AFTER_F = 3 * D_ATT + ATT_HEADS
_W_IN_TAIL = sum(_W_IN_SIZES[4:])


def _w_in_layout_kernel(w_ref, o_ref):
    x = w_ref[0]
    lane = lax.broadcasted_iota(jnp.int32, (1, LANES), 1)
    zero = jnp.zeros((x.shape[0], LANES), F32)

    f_tile = jnp.where(lane < ATT_HEADS, x[:, 3 * D_ATT:3 * D_ATT + LANES], zero)
    shift = _W_IN_AFTER_F % LANES
    tail0 = _W_IN_AFTER_F - shift
    tail = pltpu.roll(x, x.shape[1] - shift, axis=1)[:, tail0:tail0 + _W_IN_TAIL]
    o_ref[0] = jnp.concatenate([x[:, 2 * D_ATT:3 * D_ATT], f_tile, tail, x[:, :2 * D_ATT]], axis=1).astype(o_ref.dtype)


def _w_in_layout(w_in):
    depth, d, d_in = w_in.shape
    rows = min(256, d)
    width = -(-d_in // LANES) * LANES
    d_out = sum(n for _, n in _IN_COLS)
    return pl.pallas_call(
        _w_in_layout_kernel,
        grid=(depth, d // rows),
        in_specs=[pl.BlockSpec((1, rows, width), lambda l, i: (l, i, 0))],
        out_specs=pl.BlockSpec((1, rows, d_out), lambda l, i: (l, i, 0)),
        out_shape=jax.ShapeDtypeStruct((depth, d, d_out), BF16),
        compiler_params=_compiler_params(("parallel", "parallel")),
        name="w_in_layout",
    )(w_in)


def kernel(x, norm_mix_g, w_in, fgate_b, conv_w, conv_b, conv_ln_g, conv_ln_b, hgrn_lb_logits, hgrn_norm_g,
           w_out, norm_ffn_g, w_ffn_in, w_ffn_out, norm_final_g):
    b, t, d = x.shape
    depth = w_in.shape[0]
    tl = _tiles(t)
    row = lambda a: a.reshape(1, -1).astype(F32)

    tri = jnp.asarray(np.tril(np.ones((tl["tm"], tl["tm"]), np.float32)), BF16)
    sel, one = _bias_lane_constants()
    sel, one = jnp.asarray(sel, BF16), jnp.asarray(one)
    tri_h = jnp.asarray(np.tril(np.ones((HGRN_CHUNK, HGRN_CHUNK), np.float32)), BF16)
    lvl = jnp.asarray(_hgrn_level_map())
    lb_logits = hgrn_lb_logits.astype(F32)
    gf = row(norm_final_g)

    w_all = _w_in_layout(w_in)
    for l in range(depth):
        fb = row(jnp.pad(fgate_b[l], (0, LANES - ATT_HEADS)))
        cw = jnp.broadcast_to(conv_w[l][:, None, :], (CONV_WIDTH, _SUBLANES, D_CONV))
        qx, kx, va, cnv, qh, fh, ih, gh, stats = _inproj(x, row(norm_mix_g[l]), w_all, l, fb, tri, sel, one, cw,
                                                  row(conv_b[l]), row(conv_ln_g[l]), row(conv_ln_b[l]))

        att = _attention(qx, kx, va, stats, tl["tq"], tl["tk"])

        gn = row(jnp.tile(hgrn_norm_g[l], LANES // HEAD_DIM))
        hg = _hgrn(qh, fh, ih, gh, lb_logits, gn, tri_h, lvl, l, tl["tt"])

        x = _out_ffn(x.reshape(b * t, d), att.reshape(b * t, -1), cnv.reshape(b * t, -1), hg.reshape(b * t, -1),
                     w_out[l].astype(BF16), row(norm_ffn_g[l]), w_ffn_in[l].astype(BF16),
                     w_ffn_out[l].astype(BF16), gf, tl["tm"], final=(l == depth - 1)).reshape(b, t, d)
    return x
```

```python
import functools

import numpy as np
import jax
import jax.numpy as jnp
from jax import lax
from jax.experimental import pallas as pl
from jax.experimental.pallas import tpu as pltpu

F32 = jnp.float32
BF16 = jnp.bfloat16

HEAD_DIM = 64
ATT_HEADS = 6
D_ATT = ATT_HEADS * HEAD_DIM
D_CONV = 256
CONV_WIDTH = 31
HGRN_HEADS = 6
D_HGRN = HGRN_HEADS * HEAD_DIM
RMS_EPS = 1e-6
LN_EPS = 1e-5

LANES = 128
HEAD_PAIRS = D_ATT // LANES
CONV_HALO = 32
VMEM_LIMIT_BYTES = 56 * 1024 * 1024

D_ATT_X = ATT_HEADS * LANES
LOG2E = 1.4426950408889634

_IN_COLS = (("va", D_ATT), ("fa", LANES), ("cin", 2 * D_CONV), ("qh", D_HGRN), ("fh", D_HGRN),
            ("ih", D_HGRN), ("gh", D_HGRN), ("qa", D_ATT), ("ka", D_ATT))
_IN_TILES = tuple((name, i) for name, n in _IN_COLS for i in range(n // LANES))
MXU_COLS = 2 * LANES
_IN_CHUNK_ORDER = (2, 3, 1, 0) + tuple(range(4, len(_IN_TILES) * LANES // MXU_COLS))
_IN_OUTPUTS = (("qx", D_ATT_X, BF16), ("kx", D_ATT_X, BF16), ("va", D_ATT, BF16), ("cnv", D_CONV, BF16),
               ("qh", D_HGRN, F32), ("fh", D_HGRN, F32), ("ih", D_HGRN, BF16), ("gh", D_HGRN, F32))
_N_SPLIT = 3


def _nt_dot(a, b):
    return lax.dot_general(a, b, (((1,), (1,)), ((), ())), preferred_element_type=F32)


def _dot(a, b):
    return jnp.dot(a, b, preferred_element_type=F32)


def _rmsnorm_rows(x, g):
    return x * lax.rsqrt(jnp.mean(x * x, axis=-1, keepdims=True) + RMS_EPS) * g


def _compiler_params(semantics):
    return pltpu.CompilerParams(dimension_semantics=semantics, vmem_limit_bytes=VMEM_LIMIT_BYTES)


def _resident(shape):
    nd = len(shape)
    return pl.BlockSpec(shape, lambda *_: (0,) * nd, pipeline_mode=pl.Buffered(1))


def _split_bf16(x, n):
    terms = []
    for _ in range(n - 1):
        t = x.astype(BF16)
        terms.append(t)
        x = x - t.astype(F32)
    terms.append(x.astype(BF16))
    return terms


_BIAS_LANES = 2 * _N_SPLIT
_BIAS_STRIDE = 8


def _bias_lane_constants():
    sel = np.zeros((_N_SPLIT * LANES, 2 * LANES), np.float32)
    one = np.zeros((1, 2 * LANES), np.float32)
    for h in range(ATT_HEADS):
        base = h * _BIAS_STRIDE
        for i in range(_N_SPLIT):
            sel[i * LANES + h, base + i] = 1.0
            one[0, base + _N_SPLIT + i] = 1.0
            one[0, LANES + base + i] = 1.0
            sel[i * LANES + h, LANES + base + _N_SPLIT + i] = -1.0
    return sel, one


_STATS_ROWS = (tuple(("qa", p) for p in range(HEAD_PAIRS)) + tuple(("ka", p) for p in range(HEAD_PAIRS))
               + ("c_first", "c_last"))
_CONV_ROWS = 16
_SUBLANES = 8


def _inproj_kernel(x_ref, g_ref, w_ref, fb_ref, tri_ref, sel_ref, one_ref, cw_ref, cb_ref, lg_ref, lb_ref,
                   qx_ref, kx_ref, va_ref, cnv_ref, qh_ref, fh_ref, ih_ref, gh_ref, stats_ref,
                   carry_ref, halo_ref, hs_ref):
    tm = x_ref.shape[1]

    @pl.when(pl.program_id(1) == 0)
    def _():
        carry_ref[...] = jnp.zeros_like(carry_ref)
        halo_ref[...] = jnp.zeros_like(halo_ref)

    h = _rmsnorm_rows(x_ref[0], g_ref[...]).astype(BF16)
    lane = lax.broadcasted_iota(jnp.int32, (1, LANES), 1)
    direct = {"va": va_ref, "qh": qh_ref, "fh": fh_ref, "ih": ih_ref, "gh": gh_ref}
    cin, bias = {}, {}
    stats = {}
    same_head = (lax.broadcasted_iota(jnp.int32, (LANES, LANES), 0) // HEAD_DIM
                 == lax.broadcasted_iota(jnp.int32, (LANES, LANES), 1) // HEAD_DIM).astype(BF16)

    def forget_bias(z):
        log_f = jax.nn.log_sigmoid(z + fb_ref[...])
        tri = tri_ref[...]
        parts = _dot(tri, jnp.concatenate(_split_bf16(log_f, _N_SPLIT), axis=1))
        c = carry_ref[...] + sum(parts[:, i * LANES:(i + 1) * LANES] for i in range(_N_SPLIT))
        carry_ref[...] = c[tm - 1:tm, :]
        c2 = c * LOG2E
        stats["c_first"], stats["c_last"] = c2[0:1, :], c2[tm - 1:tm, :]
        terms = jnp.concatenate(_split_bf16(c2, _N_SPLIT), axis=1)
        bias["all"] = _dot(terms, sel_ref[...]) + one_ref[...]

    def conv_input():
        half = D_CONV // LANES
        glu = jnp.concatenate([cin[i] * jax.nn.sigmoid(cin[half + i]) for i in range(half)], axis=1)
        hs_ref[0, 0:CONV_HALO, :] = halo_ref[...]
        hs_ref[0, CONV_HALO:CONV_HALO + tm, :] = glu
        halo_ref[...] = glu[tm - CONV_HALO:, :]
        span = tm + CONV_HALO - _SUBLANES
        for s in range(1, _SUBLANES):
            hs_ref[s, 0:span, :] = hs_ref[0, s:s + span, :]

    def conv_rows(r0, after):
        first = CONV_HALO - (CONV_WIDTH - 1)
        acc = jnp.minimum(jnp.abs(after[0:_CONV_ROWS, 0:D_CONV]), 0.0)
        for tap in range(CONV_WIDTH):
            off = first + tap
            base = r0 + off // _SUBLANES * _SUBLANES
            taps = jnp.tile(cw_ref[tap], (_CONV_ROWS // _SUBLANES, 1))
            acc = acc + hs_ref[off % _SUBLANES, base:base + _CONV_ROWS, :] * taps
        acc = acc + cb_ref[...]
        mu = jnp.mean(acc, axis=-1, keepdims=True)
        xc = acc - mu
        var = jnp.mean(xc * xc, axis=-1, keepdims=True)
        y = xc * lax.rsqrt(var + LN_EPS) * lg_ref[...] + lb_ref[...]
        cnv_ref[0, r0:r0 + _CONV_ROWS, :] = (y * jax.nn.sigmoid(y)).astype(cnv_ref.dtype)

    def head_pair(name, z, pair, packed_bias, o_ref):
        norm2 = _dot((z * z).astype(BF16), same_head)
        stats[(name, pair)] = jnp.max(norm2, axis=0, keepdims=True)
        for e in range(2):
            hd = 2 * pair + e
            blk = pltpu.roll(z, HEAD_DIM, axis=1) if e else z
            own_bias = pltpu.roll(packed_bias, HEAD_DIM - hd * _BIAS_STRIDE, axis=1)
            tail = jnp.where(lane < HEAD_DIM + _BIAS_LANES, own_bias, 0.0)
            o_ref[0, :, hd * LANES:(hd + 1) * LANES] = jnp.where(lane < HEAD_DIM, blk, tail).astype(o_ref.dtype)

    def route(name, i, z):
        if name in direct:
            direct[name][0, :, i * LANES:(i + 1) * LANES] = z.astype(direct[name].dtype)
        elif name == "fa":
            forget_bias(z)
        elif name == "cin":
            cin[i] = z
            if len(cin) == 2 * D_CONV // LANES:
                conv_input()
        elif name == "qa":
            head_pair(name, z * (HEAD_DIM ** -0.5 * LOG2E), i, bias["all"][:, :LANES], qx_ref)
        elif name == "ka":
            head_pair(name, z, i, bias["all"][:, LANES:], kx_ref)

    conv_starts = list(range(0, tm, _CONV_ROWS))
    conv_slots = len(_IN_CHUNK_ORDER) - _IN_CHUNK_ORDER.index(3) - 1
    done = 0
    for pos, chunk in enumerate(_IN_CHUNK_ORDER):
        z = _dot(h, w_ref[:, chunk * MXU_COLS:(chunk + 1) * MXU_COLS])
        for e in range(MXU_COLS // LANES):
            name, i = _IN_TILES[chunk * (MXU_COLS // LANES) + e]
            route(name, i, z[:, e * LANES:(e + 1) * LANES])
        slot = pos - (len(_IN_CHUNK_ORDER) - conv_slots)
        if slot >= 0:
            upto = -(-(slot + 1) * len(conv_starts) // conv_slots)
            for r0 in conv_starts[done:upto]:
                conv_rows(r0, z)
            done = upto
    stats_ref[0, 0] = jnp.concatenate([stats[k] for k in _STATS_ROWS], axis=0)


def _inproj(x, g, w_all, layer, fb, tri, sel, one, cw, cb, ln_g, ln_b):
    b, t, d = x.shape
    tm = tri.shape[0]
    tile = lambda n: pl.BlockSpec((1, tm, n), lambda i, j: (i, j, 0))
    w_spec = pl.BlockSpec((None,) + w_all.shape[1:], lambda i, j: (layer, 0, 0), pipeline_mode=pl.Buffered(1))
    consts = (g, w_all, fb, tri, sel, one, cw, cb, ln_g, ln_b)
    return pl.pallas_call(
        _inproj_kernel,
        grid=(b, t // tm),
        in_specs=[tile(d)] + [w_spec if a is w_all else _resident(a.shape) for a in consts],
        out_specs=[tile(n) for _, n, _ in _IN_OUTPUTS]
        + [pl.BlockSpec((1, 1, _SUBLANES, LANES), lambda i, j: (i, j, 0, 0))],
        out_shape=[jax.ShapeDtypeStruct((b, t, n), dt) for _, n, dt in _IN_OUTPUTS]
        + [jax.ShapeDtypeStruct((b, t // tm, _SUBLANES, LANES), F32)],
        scratch_shapes=[pltpu.VMEM((1, LANES), F32), pltpu.VMEM((CONV_HALO, D_CONV), F32),
                        pltpu.VMEM((_SUBLANES, tm + CONV_HALO, D_CONV), F32)],
        compiler_params=_compiler_params(("parallel", "arbitrary")),
        name="inproj",
    )(x, *consts)


_EXP2_ZERO = -160.0
_NORM_SLACK = 1.02


def _kept_windows(stats):
    heads = slice(0, ATT_HEADS)

    def norms(name):
        first = _STATS_ROWS.index((name, 0))
        sq = stats[:, :, first:first + HEAD_PAIRS, ::HEAD_DIM]
        return jnp.sqrt(sq.reshape(sq.shape[0], sq.shape[1], ATT_HEADS)) * _NORM_SLACK

    qn, kn = norms("qa"), norms("ka")
    c_first = stats[:, :, _STATS_ROWS.index("c_first"), heads]
    c_last = stats[:, :, _STATS_ROWS.index("c_last"), heads]
    bound = (qn[:, :, None] * kn[:, None, :] + c_first[:, :, None] - c_last[:, None, :]
             + (qn * kn)[:, :, None])
    tile = np.arange(stats.shape[1])
    keep = jnp.any(bound > _EXP2_ZERO, axis=-1) | (tile[:, None] == tile[None, :])
    first_kept = jnp.min(jnp.where(keep, tile[None, None, :], tile[None, :, None]), axis=-1)
    window = jnp.max(tile[None, :] - first_kept) + 1
    return first_kept.astype(jnp.int32), window.astype(jnp.int32).reshape(1)


def _attn_kernel(first_ref, window_ref, q_ref, k_ref, v_ref, o_ref, m_ref, acc_ref, *, tile):
    bb, i, slot = pl.program_id(0), pl.program_id(1), pl.program_id(2)
    j = i - (window_ref[0] - 1) + slot
    first = first_ref[bb, i]
    lane = lax.broadcasted_iota(jnp.int32, (1, LANES), 1)

    @pl.when(j == first)
    def _():
        m_ref[...] = jnp.full_like(m_ref, -jnp.inf)
        acc_ref[...] = jnp.zeros_like(acc_ref)

    def scores(h):
        lanes = slice(h * LANES, (h + 1) * LANES)
        return _nt_dot(q_ref[0, :, lanes], k_ref[0, :, lanes])

    def block(diagonal):
        if diagonal:
            row = lax.broadcasted_iota(jnp.int32, (tile, tile), 0)
            col = lax.broadcasted_iota(jnp.int32, (tile, tile), 1)
            keep = col <= row
        s_next = scores(0)
        for h in range(ATT_HEADS):
            s = s_next
            if h + 1 < ATT_HEADS:
                s_next = scores(h + 1)
            if diagonal:
                s = jnp.where(keep, s, -jnp.inf)
            pair = slice((h // 2) * LANES, (h // 2 + 1) * LANES)
            own = (lane < HEAD_DIM) if h % 2 == 0 else (lane >= HEAD_DIM)
            v2 = v_ref[0, :, pair]
            v1 = jnp.where(own, v2, jnp.ones_like(v2))
            m_prev = m_ref[h]
            m_new = jnp.maximum(m_prev, jnp.max(s, axis=-1, keepdims=True))
            p = jnp.exp2(s - jnp.tile(m_new, (1, tile // LANES)))
            acc_ref[h] = jnp.exp2(m_prev - m_new) * acc_ref[h] + _dot(p.astype(BF16), v1)
            m_ref[h] = m_new

    @pl.when((j >= first) & (j < i))
    def _():
        block(False)

    @pl.when(j == i)
    def _():
        block(True)
        for p in range(HEAD_PAIRS):
            a0, a1 = acc_ref[2 * p], acc_ref[2 * p + 1]
            o0 = a0 / pltpu.roll(a0, HEAD_DIM, axis=1)
            o1 = a1 / pltpu.roll(a1, HEAD_DIM, axis=1)
            o_ref[0, :, p * LANES:(p + 1) * LANES] = jnp.where(lane < HEAD_DIM, o0, o1).astype(o_ref.dtype)


def _attention(qx, kx, va, stats, tile):
    b, t, _ = va.shape
    first_kept, window = _kept_windows(stats)
    q_map = lambda bb, i, s, first, window: (bb, i, 0)
    k_map = lambda bb, i, s, first, window: (bb, jnp.maximum(i - (window[0] - 1) + s, first[bb, i]), 0)
    grid_spec = pltpu.PrefetchScalarGridSpec(
        num_scalar_prefetch=2,
        grid=(b, t // tile, window[0]),
        in_specs=[pl.BlockSpec((1, tile, D_ATT_X), q_map),
                  pl.BlockSpec((1, tile, D_ATT_X), k_map),
                  pl.BlockSpec((1, tile, D_ATT), k_map)],
        out_specs=pl.BlockSpec((1, tile, D_ATT), q_map),
        scratch_shapes=[pltpu.VMEM((ATT_HEADS, tile, LANES), F32), pltpu.VMEM((ATT_HEADS, tile, LANES), F32)],
    )
    return pl.pallas_call(
        functools.partial(_attn_kernel, tile=tile),
        grid_spec=grid_spec,
        out_shape=jax.ShapeDtypeStruct((b, t, D_ATT), BF16),
        compiler_params=_compiler_params(("parallel", "arbitrary", "arbitrary")),
        name="fox_attention",
    )(first_kept, window, qx, kx, va)


HGRN_CHUNK = 128
_HGRN_LEVELS = HGRN_CHUNK.bit_length()


def _hgrn_level_map():
    t = np.arange(HGRN_CHUNK)[:, None]
    s = np.arange(HGRN_CHUNK)[None, :]
    high_bit = np.floor(np.log2(np.maximum(t ^ s, 1))).astype(np.int32)
    level = np.where(t == s, 0, np.where(s < t, high_bit + 1, -1)).astype(np.int32)
    return np.concatenate([level, level], axis=1)


def _hgrn_kernel(q_ref, f_ref, v_ref, g_ref, lbl_ref, gn_ref, tri_ref, lvl_ref, o_ref, st_ref,
                 *, layer, chunks):
    c = HGRN_CHUNK

    @pl.when(pl.program_id(1) == 0)
    def _():
        st_ref[...] = jnp.zeros_like(st_ref)

    z = lbl_ref[...]
    e = jnp.exp(z - jnp.max(z, axis=0, keepdims=True))
    if layer > 0:
        lb_all = jnp.sum(e[1:layer + 1], axis=0, keepdims=True) / jnp.sum(e, axis=0, keepdims=True)
    else:
        lb_all = jnp.zeros((1, D_HGRN), F32)

    lane = lax.broadcasted_iota(jnp.int32, (1, LANES), 1)
    head0 = lane < HEAD_DIM
    row = lax.broadcasted_iota(jnp.int32, (c, LANES), 0)
    level = lvl_ref[...]
    vi = lax.broadcasted_iota(jnp.int32, (LANES, LANES), 0) // HEAD_DIM
    di = lax.broadcasted_iota(jnp.int32, (LANES, LANES), 1) // HEAD_DIM
    same_head = vi == di

    def head_split(x):
        return jnp.where(head0, x, 0.0), jnp.where(head0, 0.0, x)

    def head_stack(x0, x1):
        return jnp.concatenate([x0, x1], axis=0).astype(BF16)

    def start(rows, p):
        cols = slice(p * LANES, (p + 1) * LANES)
        lb = lb_all[:, cols]
        q = q_ref[0, rows, cols]
        v2 = v_ref[0, rows, cols].astype(F32)
        f = lb + (1.0 - lb) * jax.nn.sigmoid(f_ref[0, rows, cols])
        kk = 1.0 - f
        g = jnp.log(f) * LOG2E
        b2 = _dot(tri_ref[...], jnp.concatenate(_split_bf16(g, 2), axis=1))
        b_incl = b2[:, :LANES] + b2[:, LANES:]
        kk0, kk1 = head_split(kk)
        return dict(cols=cols, p=p, q=q, v2=v2, kk=kk, kk0=kk0, kk1=kk1, g=g, b_incl=b_incl,
                    kk_stack=head_stack(kk0, kk1),
                    a=jnp.zeros((c, 2 * c), F32), seg_end=b_incl)

    def level_step(s, lvl):
        q, kk, b_incl = s["q"], s["kk"], s["b_incl"]
        m = 1 << max(lvl - 1, 0)
        if lvl == 0:
            qt, kt = q, s["kk_stack"]
        elif lvl == 1:
            qt, kt = q * jnp.exp2(s["g"]), s["kk_stack"]
        else:
            half = m // 2
            seg_end = s["seg_end"]
            seg_end = jnp.where((row & half) == 0, pltpu.roll(seg_end, c - half, axis=0), seg_end)
            prev_end = jnp.where(row >= m, pltpu.roll(seg_end, m, axis=0), 0.0)
            qt = q * jnp.exp2(b_incl - prev_end)
            later = jnp.exp2(seg_end - b_incl)
            kt = head_stack(s["kk0"] * later, s["kk1"] * later)
            s["seg_end"] = seg_end
        s["a"] = jnp.where(level == lvl, _nt_dot(qt.astype(BF16), kt), s["a"])

    def finish(s, rows):
        q, kk, v2, b_incl, p, cols = s["q"], s["kk"], s["v2"], s["b_incl"], s["p"], s["cols"]
        o = _dot(s["a"].astype(BF16), head_stack(*head_split(v2)))
        st = st_ref[p]
        o = o + _nt_dot((q * jnp.exp2(b_incl)).astype(BF16), st.astype(BF16))
        b_end = b_incl[c - 1:c, :]
        upd = _dot(v2.T.astype(BF16), (kk * jnp.exp2(b_end - b_incl)).astype(BF16))
        st_ref[p] = st * jnp.exp2(b_end) + jnp.where(same_head, upd, 0.0)

        o2 = o * o
        ms0 = jnp.sum(jnp.where(head0, o2, 0.0), axis=-1, keepdims=True) * (1.0 / HEAD_DIM)
        ms1 = jnp.sum(jnp.where(head0, 0.0, o2), axis=-1, keepdims=True) * (1.0 / HEAD_DIM)
        inv = jnp.where(head0, lax.rsqrt(ms0 + RMS_EPS), lax.rsqrt(ms1 + RMS_EPS))
        gate = g_ref[0, rows, cols]
        o_ref[0, rows, cols] = (o * inv * gn_ref[...] * (gate * jax.nn.sigmoid(gate))).astype(o_ref.dtype)

    for ch in range(chunks):
        rows = slice(ch * c, (ch + 1) * c)
        states = [start(rows, p) for p in range(HEAD_PAIRS)]
        for lvl in range(_HGRN_LEVELS):
            for s in states:
                level_step(s, lvl)
        for s in states:
            finish(s, rows)


def _hgrn(qh, fh, ih, gh, lb_logits, gn, tri, lvl, layer, tt):
    b, t, _ = qh.shape
    tile = pl.BlockSpec((1, tt, D_HGRN), lambda i, j: (i, j, 0))
    return pl.pallas_call(
        functools.partial(_hgrn_kernel, layer=layer, chunks=tt // HGRN_CHUNK),
        grid=(b, t // tt),
        in_specs=[tile, tile, tile, tile, _resident(lb_logits.shape),
                  _resident(gn.shape), _resident(tri.shape), _resident(lvl.shape)],
        out_specs=tile,
        out_shape=jax.ShapeDtypeStruct((b, t, D_HGRN), BF16),
        scratch_shapes=[pltpu.VMEM((HEAD_PAIRS, LANES, LANES), F32)],
        compiler_params=_compiler_params(("parallel", "arbitrary")),
        name="hgrn2",
    )(qh, fh, ih, gh, lb_logits, gn, tri, lvl)


_FFN_COLS = 256


def _ffn_kernel(x_ref, att_ref, cnv_ref, hg_ref, wo_ref, g_ref, wi_ref, w2_ref, gf_ref, o_ref, a_ref,
                *, d_ff, final):
    mix = jnp.concatenate([att_ref[...], cnv_ref[...], hg_ref[...]], axis=1)
    x1 = x_ref[...] + _dot(mix, wo_ref[...])
    h = _rmsnorm_rows(x1, g_ref[...]).astype(BF16)
    for c0 in range(0, d_ff, _FFN_COLS):
        gate = _dot(h, wi_ref[:, c0:c0 + _FFN_COLS])
        up = _dot(h, wi_ref[:, d_ff + c0:d_ff + c0 + _FFN_COLS])
        a_ref[:, c0:c0 + _FFN_COLS] = (gate * jax.nn.sigmoid(gate) * up).astype(BF16)
    y = x1 + _dot(a_ref[...], w2_ref[...])
    if final:
        y = _rmsnorm_rows(y, gf_ref[...])
    o_ref[...] = y


def _out_ffn(x2d, att, cnv, hg, wo, g, wi, w2, gf, tm, final):
    m, d = x2d.shape
    d_ff = w2.shape[0]
    row = lambda n: pl.BlockSpec((tm, n), lambda i: (i, 0))
    return pl.pallas_call(
        functools.partial(_ffn_kernel, d_ff=d_ff, final=final),
        grid=(m // tm,),
        in_specs=[row(d), row(D_ATT), row(D_CONV), row(D_HGRN), _resident(wo.shape), _resident(g.shape),
                  _resident(wi.shape), _resident(w2.shape), _resident(gf.shape)],
        out_specs=row(d),
        out_shape=jax.ShapeDtypeStruct((m, d), F32),
        scratch_shapes=[pltpu.VMEM((tm, d_ff), BF16)],
        compiler_params=_compiler_params(("parallel",)),
        name="out_ffn",
    )(x2d, att, cnv, hg, wo, g, wi, w2, gf)


def _tiles(t):
    return dict(tm=min(512, t), tt=min(256, t))


_W_IN_SIZES = (D_ATT, D_ATT, D_ATT, ATT_HEADS, 2 * D_CONV, D_HGRN, D_HGRN, D_HGRN, D_HGRN)
_W_IN_AFTER_F = 3 * D_ATT + ATT_HEADS
_W_IN_TAIL = sum(_W_IN_SIZES[4:])


def _w_in_layout_kernel(w_ref, o_ref):
    x = w_ref[0]
    lane = lax.broadcasted_iota(jnp.int32, (1, LANES), 1)
    zero = jnp.zeros((x.shape[0], LANES), F32)

    f_tile = jnp.where(lane < ATT_HEADS, x[:, 3 * D_ATT:3 * D_ATT + LANES], zero)
    shift = _W_IN_AFTER_F % LANES
    tail0 = _W_IN_AFTER_F - shift
    tail = pltpu.roll(x, x.shape[1] - shift, axis=1)[:, tail0:tail0 + _W_IN_TAIL]
    o_ref[0] = jnp.concatenate([x[:, 2 * D_ATT:3 * D_ATT], f_tile, tail, x[:, :2 * D_ATT]], axis=1).astype(o_ref.dtype)


def _w_in_layout(w_in):
    depth, d, d_in = w_in.shape
    rows = min(256, d)
    width = -(-d_in // LANES) * LANES
    d_out = sum(n for _, n in _IN_COLS)
    return pl.pallas_call(
        _w_in_layout_kernel,
        grid=(depth, d // rows),
        in_specs=[pl.BlockSpec((1, rows, width), lambda l, i: (l, i, 0))],
        out_specs=pl.BlockSpec((1, rows, d_out), lambda l, i: (l, i, 0)),
        out_shape=jax.ShapeDtypeStruct((depth, d, d_out), BF16),
        compiler_params=_compiler_params(("parallel", "parallel")),
        name="w_in_layout",
    )(w_in)


def kernel(x, norm_mix_g, w_in, fgate_b, conv_w, conv_b, conv_ln_g, conv_ln_b, hgrn_lb_logits, hgrn_norm_g,
           w_out, norm_ffn_g, w_ffn_in, w_ffn_out, norm_final_g):
    b, t, d = x.shape
    depth = w_in.shape[0]
    tl = _tiles(t)
    row = lambda a: a.reshape(1, -1).astype(F32)

    tri = jnp.asarray(np.tril(np.ones((tl["tm"], tl["tm"]), np.float32)), BF16)
    sel, one = _bias_lane_constants()
    sel, one = jnp.asarray(sel, BF16), jnp.asarray(one)
    tri_h = jnp.asarray(np.tril(np.ones((HGRN_CHUNK, HGRN_CHUNK), np.float32)), BF16)
    lvl = jnp.asarray(_hgrn_level_map())
    lb_logits = hgrn_lb_logits.astype(F32)
    gf = row(norm_final_g)

    w_all = _w_in_layout(w_in)
    for l in range(depth):
        fb = row(jnp.pad(fgate_b[l], (0, LANES - ATT_HEADS)))
        cw = jnp.broadcast_to(conv_w[l][:, None, :], (CONV_WIDTH, _SUBLANES, D_CONV))
        qx, kx, va, cnv, qh, fh, ih, gh, stats = _inproj(x, row(norm_mix_g[l]), w_all, l, fb, tri, sel, one, cw,
                                                  row(conv_b[l]), row(conv_ln_g[l]), row(conv_ln_b[l]))

        att = _attention(qx, kx, va, stats, tl["tm"])

        gn = row(jnp.tile(hgrn_norm_g[l], LANES // HEAD_DIM))
        hg = _hgrn(qh, fh, ih, gh, lb_logits, gn, tri_h, lvl, l, tl["tt"])

        x = _out_ffn(x.reshape(b * t, d), att.reshape(b * t, -1), cnv.reshape(b * t, -1), hg.reshape(b * t, -1),
                     w_out[l].astype(BF16), row(norm_ffn_g[l]), w_ffn_in[l].astype(BF16),
                     w_ffn_out[l].astype(BF16), gf, tl["tm"], final=(l == depth - 1)).reshape(b, t, d)
    return x
```

```python
import functools

import numpy as np
import jax
import jax.numpy as jnp
from jax import lax
from jax.experimental import pallas as pl
from jax.experimental.pallas import tpu as pltpu

F32 = jnp.float32
BF16 = jnp.bfloat16

HEAD_DIM = 64
ATT_HEADS = 6
D_ATT = ATT_HEADS * HEAD_DIM
D_CONV = 256
CONV_WIDTH = 31
HGRN_HEADS = 6
D_HGRN = HGRN_HEADS * HEAD_DIM
RMS_EPS = 1e-6
LN_EPS = 1e-5

LANES = 128
HEAD_PAIRS = D_ATT // LANES
CONV_HALO = 32
VMEM_LIMIT_BYTES = 56 * 1024 * 1024

D_ATT_X = ATT_HEADS * LANES
LOG2E = 1.4426950408889634

_IN_COLS = (("va", D_ATT), ("fa", LANES), ("cin", 2 * D_CONV), ("qh", D_HGRN), ("fh", D_HGRN),
            ("ih", D_HGRN), ("gh", D_HGRN), ("qa", D_ATT), ("ka", D_ATT))
_IN_TILES = tuple((name, i) for name, n in _IN_COLS for i in range(n // LANES))
MXU_COLS = 2 * LANES
_IN_CHUNK_ORDER = (2, 3, 1, 0) + tuple(range(4, len(_IN_TILES) * LANES // MXU_COLS))
_IN_OUTPUTS = (("qx", D_ATT_X, BF16), ("kx", D_ATT_X, BF16), ("va", D_ATT, BF16), ("cnv", D_CONV, BF16),
               ("qh", D_HGRN, F32), ("fh", D_HGRN, F32), ("ih", D_HGRN, BF16), ("gh", D_HGRN, F32))
_N_SPLIT = 3


def _nt_dot(a, b):
    return lax.dot_general(a, b, (((1,), (1,)), ((), ())), preferred_element_type=F32)


def _dot(a, b):
    return jnp.dot(a, b, preferred_element_type=F32)


def _rmsnorm_rows(x, g):
    return x * lax.rsqrt(jnp.mean(x * x, axis=-1, keepdims=True) + RMS_EPS) * g


def _compiler_params(semantics):
    return pltpu.CompilerParams(dimension_semantics=semantics, vmem_limit_bytes=VMEM_LIMIT_BYTES)


def _resident(shape):
    nd = len(shape)
    return pl.BlockSpec(shape, lambda *_: (0,) * nd, pipeline_mode=pl.Buffered(1))


def _layer_weight(w_all, layer):
    return pl.BlockSpec((None,) + w_all.shape[1:], lambda *_: (layer, 0, 0), pipeline_mode=pl.Buffered(1))


def _split_bf16(x, n):
    terms = []
    for _ in range(n - 1):
        t = x.astype(BF16)
        terms.append(t)
        x = x - t.astype(F32)
    terms.append(x.astype(BF16))
    return terms


_BIAS_LANES = 2 * _N_SPLIT
_BIAS_STRIDE = 8


def _bias_lane_constants():
    sel = np.zeros((_N_SPLIT * LANES, 2 * LANES), np.float32)
    one = np.zeros((1, 2 * LANES), np.float32)
    for h in range(ATT_HEADS):
        base = h * _BIAS_STRIDE
        for i in range(_N_SPLIT):
            sel[i * LANES + h, base + i] = 1.0
            one[0, base + _N_SPLIT + i] = 1.0
            one[0, LANES + base + i] = 1.0
            sel[i * LANES + h, LANES + base + _N_SPLIT + i] = -1.0
    return sel, one


_STATS_ROWS = (tuple(("qa", p) for p in range(HEAD_PAIRS)) + tuple(("ka", p) for p in range(HEAD_PAIRS))
               + ("c_first", "c_last"))
_CONV_ROWS = 16
_SUBLANES = 8


def _inproj_kernel(x_ref, g_ref, w_ref, fb_ref, tri_ref, sel_ref, one_ref, cw_ref, cb_ref, lg_ref, lb_ref,
                   qx_ref, kx_ref, va_ref, cnv_ref, qh_ref, fh_ref, ih_ref, gh_ref, stats_ref,
                   carry_ref, halo_ref, hs_ref):
    tm = x_ref.shape[1]

    @pl.when(pl.program_id(1) == 0)
    def _():
        carry_ref[...] = jnp.zeros_like(carry_ref)
        halo_ref[...] = jnp.zeros_like(halo_ref)

    h = _rmsnorm_rows(x_ref[0], g_ref[...]).astype(BF16)
    lane = lax.broadcasted_iota(jnp.int32, (1, LANES), 1)
    direct = {"va": va_ref, "qh": qh_ref, "fh": fh_ref, "ih": ih_ref, "gh": gh_ref}
    cin, bias = {}, {}
    stats = {}
    same_head = (lax.broadcasted_iota(jnp.int32, (LANES, LANES), 0) // HEAD_DIM
                 == lax.broadcasted_iota(jnp.int32, (LANES, LANES), 1) // HEAD_DIM).astype(BF16)

    def forget_bias(z):
        log_f = jax.nn.log_sigmoid(z + fb_ref[...])
        tri = tri_ref[...]
        parts = _dot(tri, jnp.concatenate(_split_bf16(log_f, _N_SPLIT), axis=1))
        c = carry_ref[...] + sum(parts[:, i * LANES:(i + 1) * LANES] for i in range(_N_SPLIT))
        carry_ref[...] = c[tm - 1:tm, :]
        c2 = c * LOG2E
        stats["c_first"], stats["c_last"] = c2[0:1, :], c2[tm - 1:tm, :]
        terms = jnp.concatenate(_split_bf16(c2, _N_SPLIT), axis=1)
        bias["all"] = _dot(terms, sel_ref[...]) + one_ref[...]

    def conv_input():
        half = D_CONV // LANES
        glu = jnp.concatenate([cin[i] * jax.nn.sigmoid(cin[half + i]) for i in range(half)], axis=1)
        hs_ref[0, 0:CONV_HALO, :] = halo_ref[...]
        hs_ref[0, CONV_HALO:CONV_HALO + tm, :] = glu
        halo_ref[...] = glu[tm - CONV_HALO:, :]
        span = tm + CONV_HALO - _SUBLANES
        for s in range(1, _SUBLANES):
            hs_ref[s, 0:span, :] = hs_ref[0, s:s + span, :]

    def conv_rows(r0, after):
        first = CONV_HALO - (CONV_WIDTH - 1)
        acc = jnp.minimum(jnp.abs(after[0:_CONV_ROWS, 0:D_CONV]), 0.0)
        for tap in range(CONV_WIDTH):
            off = first + tap
            base = r0 + off // _SUBLANES * _SUBLANES
            taps = jnp.tile(cw_ref[tap], (_CONV_ROWS // _SUBLANES, 1))
            acc = acc + hs_ref[off % _SUBLANES, base:base + _CONV_ROWS, :] * taps
        acc = acc + cb_ref[...]
        mu = jnp.mean(acc, axis=-1, keepdims=True)
        xc = acc - mu
        var = jnp.mean(xc * xc, axis=-1, keepdims=True)
        y = xc * lax.rsqrt(var + LN_EPS) * lg_ref[...] + lb_ref[...]
        cnv_ref[0, r0:r0 + _CONV_ROWS, :] = (y * jax.nn.sigmoid(y)).astype(cnv_ref.dtype)

    def head_pair(name, z, pair, packed_bias, o_ref):
        norm2 = _dot((z * z).astype(BF16), same_head)
        stats[(name, pair)] = jnp.max(norm2, axis=0, keepdims=True)
        for e in range(2):
            hd = 2 * pair + e
            blk = pltpu.roll(z, HEAD_DIM, axis=1) if e else z
            own_bias = pltpu.roll(packed_bias, HEAD_DIM - hd * _BIAS_STRIDE, axis=1)
            tail = jnp.where(lane < HEAD_DIM + _BIAS_LANES, own_bias, 0.0)
            o_ref[0, :, hd * LANES:(hd + 1) * LANES] = jnp.where(lane < HEAD_DIM, blk, tail).astype(o_ref.dtype)

    def route(name, i, z):
        if name in direct:
            direct[name][0, :, i * LANES:(i + 1) * LANES] = z.astype(direct[name].dtype)
        elif name == "fa":
            forget_bias(z)
        elif name == "cin":
            cin[i] = z
            if len(cin) == 2 * D_CONV // LANES:
                conv_input()
        elif name == "qa":
            head_pair(name, z * (HEAD_DIM ** -0.5 * LOG2E), i, bias["all"][:, :LANES], qx_ref)
        elif name == "ka":
            head_pair(name, z, i, bias["all"][:, LANES:], kx_ref)

    conv_starts = list(range(0, tm, _CONV_ROWS))
    conv_slots = len(_IN_CHUNK_ORDER) - _IN_CHUNK_ORDER.index(3) - 1
    done = 0
    for pos, chunk in enumerate(_IN_CHUNK_ORDER):
        z = _dot(h, w_ref[:, chunk * MXU_COLS:(chunk + 1) * MXU_COLS])
        for e in range(MXU_COLS // LANES):
            name, i = _IN_TILES[chunk * (MXU_COLS // LANES) + e]
            route(name, i, z[:, e * LANES:(e + 1) * LANES])
        slot = pos - (len(_IN_CHUNK_ORDER) - conv_slots)
        if slot >= 0:
            upto = -(-(slot + 1) * len(conv_starts) // conv_slots)
            for r0 in conv_starts[done:upto]:
                conv_rows(r0, z)
            done = upto
    stats_ref[0, 0] = jnp.concatenate([stats[k] for k in _STATS_ROWS], axis=0)


def _inproj(x, g, w_all, layer, fb, tri, sel, one, cw, cb, ln_g, ln_b):
    b, t, d = x.shape
    tm = tri.shape[0]
    tile = lambda n: pl.BlockSpec((1, tm, n), lambda i, j: (i, j, 0))
    consts = (g, w_all, fb, tri, sel, one, cw, cb, ln_g, ln_b)
    return pl.pallas_call(
        _inproj_kernel,
        grid=(b, t // tm),
        in_specs=[tile(d)] + [_layer_weight(a, layer) if a is w_all else _resident(a.shape) for a in consts],
        out_specs=[tile(n) for _, n, _ in _IN_OUTPUTS]
        + [pl.BlockSpec((1, 1, _SUBLANES, LANES), lambda i, j: (i, j, 0, 0))],
        out_shape=[jax.ShapeDtypeStruct((b, t, n), dt) for _, n, dt in _IN_OUTPUTS]
        + [jax.ShapeDtypeStruct((b, t // tm, _SUBLANES, LANES), F32)],
        scratch_shapes=[pltpu.VMEM((1, LANES), F32), pltpu.VMEM((CONV_HALO, D_CONV), F32),
                        pltpu.VMEM((_SUBLANES, tm + CONV_HALO, D_CONV), F32)],
        compiler_params=_compiler_params(("parallel", "arbitrary")),
        name="inproj",
    )(x, *consts)


_EXP2_ZERO = -160.0
_NORM_SLACK = 1.02


def _kept_windows(stats):
    heads = slice(0, ATT_HEADS)

    def norms(name):
        first = _STATS_ROWS.index((name, 0))
        sq = stats[:, :, first:first + HEAD_PAIRS, ::HEAD_DIM]
        return jnp.sqrt(sq.reshape(sq.shape[0], sq.shape[1], ATT_HEADS)) * _NORM_SLACK

    qn, kn = norms("qa"), norms("ka")
    c_first = stats[:, :, _STATS_ROWS.index("c_first"), heads]
    c_last = stats[:, :, _STATS_ROWS.index("c_last"), heads]
    bound = (qn[:, :, None] * kn[:, None, :] + c_first[:, :, None] - c_last[:, None, :]
             + (qn * kn)[:, :, None])
    tile = np.arange(stats.shape[1])
    keep = jnp.any(bound > _EXP2_ZERO, axis=-1) | (tile[:, None] == tile[None, :])
    first_kept = jnp.min(jnp.where(keep, tile[None, None, :], tile[None, :, None]), axis=-1)
    window = jnp.max(tile[None, :] - first_kept) + 1
    return first_kept.astype(jnp.int32), window.astype(jnp.int32).reshape(1)


def _attn_kernel(first_ref, window_ref, q_ref, k_ref, v_ref, o_ref, m_ref, acc_ref, *, tile):
    bb, i, slot = pl.program_id(0), pl.program_id(1), pl.program_id(2)
    j = i - (window_ref[0] - 1) + slot
    first = first_ref[bb, i]
    lane = lax.broadcasted_iota(jnp.int32, (1, LANES), 1)

    @pl.when(j == first)
    def _():
        m_ref[...] = jnp.full_like(m_ref, -jnp.inf)
        acc_ref[...] = jnp.zeros_like(acc_ref)

    def scores(h):
        lanes = slice(h * LANES, (h + 1) * LANES)
        return _nt_dot(q_ref[0, :, lanes], k_ref[0, :, lanes])

    def block(diagonal):
        if diagonal:
            row = lax.broadcasted_iota(jnp.int32, (tile, tile), 0)
            col = lax.broadcasted_iota(jnp.int32, (tile, tile), 1)
            keep = col <= row
        s_next = scores(0)
        for h in range(ATT_HEADS):
            s = s_next
            if h + 1 < ATT_HEADS:
                s_next = scores(h + 1)
            if diagonal:
                s = jnp.where(keep, s, -jnp.inf)
            pair = slice((h // 2) * LANES, (h // 2 + 1) * LANES)
            own = (lane < HEAD_DIM) if h % 2 == 0 else (lane >= HEAD_DIM)
            v2 = v_ref[0, :, pair]
            v1 = jnp.where(own, v2, jnp.ones_like(v2))
            m_prev = m_ref[h]
            m_new = jnp.maximum(m_prev, jnp.max(s, axis=-1, keepdims=True))
            p = jnp.exp2(s - jnp.tile(m_new, (1, tile // LANES)))
            acc_ref[h] = jnp.exp2(m_prev - m_new) * acc_ref[h] + _dot(p.astype(BF16), v1)
            m_ref[h] = m_new

    @pl.when((j >= first) & (j < i))
    def _():
        block(False)

    @pl.when(j == i)
    def _():
        block(True)
        for p in range(HEAD_PAIRS):
            a0, a1 = acc_ref[2 * p], acc_ref[2 * p + 1]
            o0 = a0 / pltpu.roll(a0, HEAD_DIM, axis=1)
            o1 = a1 / pltpu.roll(a1, HEAD_DIM, axis=1)
            o_ref[0, :, p * LANES:(p + 1) * LANES] = jnp.where(lane < HEAD_DIM, o0, o1).astype(o_ref.dtype)


def _attention(qx, kx, va, stats, tile):
    b, t, _ = va.shape
    first_kept, window = _kept_windows(stats)
    q_map = lambda bb, i, s, first, window: (bb, i, 0)
    k_map = lambda bb, i, s, first, window: (bb, jnp.maximum(i - (window[0] - 1) + s, first[bb, i]), 0)
    grid_spec = pltpu.PrefetchScalarGridSpec(
        num_scalar_prefetch=2,
        grid=(b, t // tile, window[0]),
        in_specs=[pl.BlockSpec((1, tile, D_ATT_X), q_map),
                  pl.BlockSpec((1, tile, D_ATT_X), k_map),
                  pl.BlockSpec((1, tile, D_ATT), k_map)],
        out_specs=pl.BlockSpec((1, tile, D_ATT), q_map),
        scratch_shapes=[pltpu.VMEM((ATT_HEADS, tile, LANES), F32), pltpu.VMEM((ATT_HEADS, tile, LANES), F32)],
    )
    return pl.pallas_call(
        functools.partial(_attn_kernel, tile=tile),
        grid_spec=grid_spec,
        out_shape=jax.ShapeDtypeStruct((b, t, D_ATT), BF16),
        compiler_params=_compiler_params(("parallel", "arbitrary", "arbitrary")),
        name="fox_attention",
    )(first_kept, window, qx, kx, va)


HGRN_CHUNK = 128
_HGRN_LEVELS = HGRN_CHUNK.bit_length()


def _hgrn_level_map():
    t = np.arange(HGRN_CHUNK)[:, None]
    s = np.arange(HGRN_CHUNK)[None, :]
    high_bit = np.floor(np.log2(np.maximum(t ^ s, 1))).astype(np.int32)
    level = np.where(t == s, 0, np.where(s < t, high_bit + 1, -1)).astype(np.int32)
    return np.concatenate([level, level], axis=1)


def _hgrn_kernel(q_ref, f_ref, v_ref, g_ref, lbl_ref, gn_ref, tri_ref, lvl_ref, o_ref, st_ref,
                 *, layer, chunks):
    c = HGRN_CHUNK

    @pl.when(pl.program_id(1) == 0)
    def _():
        st_ref[...] = jnp.zeros_like(st_ref)

    z = lbl_ref[...]
    e = jnp.exp(z - jnp.max(z, axis=0, keepdims=True))
    if layer > 0:
        lb_all = jnp.sum(e[1:layer + 1], axis=0, keepdims=True) / jnp.sum(e, axis=0, keepdims=True)
    else:
        lb_all = jnp.zeros((1, D_HGRN), F32)

    lane = lax.broadcasted_iota(jnp.int32, (1, LANES), 1)
    head0 = lane < HEAD_DIM
    row = lax.broadcasted_iota(jnp.int32, (c, LANES), 0)
    level = lvl_ref[...]
    vi = lax.broadcasted_iota(jnp.int32, (LANES, LANES), 0) // HEAD_DIM
    di = lax.broadcasted_iota(jnp.int32, (LANES, LANES), 1) // HEAD_DIM
    same_head = vi == di

    def head_split(x):
        return jnp.where(head0, x, 0.0), jnp.where(head0, 0.0, x)

    def head_stack(x0, x1):
        return jnp.concatenate([x0, x1], axis=0).astype(BF16)

    def start(rows, p):
        cols = slice(p * LANES, (p + 1) * LANES)
        lb = lb_all[:, cols]
        q = q_ref[0, rows, cols]
        v2 = v_ref[0, rows, cols].astype(F32)
        f = lb + (1.0 - lb) * jax.nn.sigmoid(f_ref[0, rows, cols])
        kk = 1.0 - f
        g = jnp.log(f) * LOG2E
        b2 = _dot(tri_ref[...], jnp.concatenate(_split_bf16(g, 2), axis=1))
        b_incl = b2[:, :LANES] + b2[:, LANES:]
        kk_stack = head_stack(*head_split(kk))
        return dict(cols=cols, p=p, q=q, v2=v2, kk=kk, q16=q.astype(BF16), g=g, b_incl=b_incl, kk_stack=kk_stack,
                    a=jnp.zeros((c, 2 * c), BF16), seg_end=b_incl)

    def level_step(s, lvl):
        q16, b_incl = s["q16"], s["b_incl"]
        m = 1 << max(lvl - 1, 0)
        if lvl == 0:
            qt, kt = q16, s["kk_stack"]
        elif lvl == 1:
            qt, kt = q16 * jnp.exp2(s["g"]).astype(BF16), s["kk_stack"]
        else:
            half = m // 2
            seg_end = s["seg_end"]
            seg_end = jnp.where((row & half) == 0, pltpu.roll(seg_end, c - half, axis=0), seg_end)
            prev_end = jnp.where(row >= m, pltpu.roll(seg_end, m, axis=0), 0.0)
            qt = q16 * jnp.exp2(b_incl - prev_end).astype(BF16)
            later = jnp.exp2(seg_end - b_incl).astype(BF16)
            kt = s["kk_stack"] * jnp.concatenate([later, later], axis=0)
            s["seg_end"] = seg_end
        s["a"] = jnp.where(level == lvl, _nt_dot(qt, kt).astype(BF16), s["a"])

    def finish(s, rows):
        q, kk, v2, b_incl, p, cols = s["q"], s["kk"], s["v2"], s["b_incl"], s["p"], s["cols"]
        o = _dot(s["a"], head_stack(*head_split(v2)))
        st = st_ref[p]
        o = o + _nt_dot((q * jnp.exp2(b_incl)).astype(BF16), st.astype(BF16))
        b_end = b_incl[c - 1:c, :]
        upd = _dot(v2.T.astype(BF16), (kk * jnp.exp2(b_end - b_incl)).astype(BF16))
        st_ref[p] = st * jnp.exp2(b_end) + jnp.where(same_head, upd, 0.0)

        o2 = o * o
        ms0 = jnp.sum(jnp.where(head0, o2, 0.0), axis=-1, keepdims=True) * (1.0 / HEAD_DIM)
        ms1 = jnp.sum(jnp.where(head0, 0.0, o2), axis=-1, keepdims=True) * (1.0 / HEAD_DIM)
        inv = jnp.where(head0, lax.rsqrt(ms0 + RMS_EPS), lax.rsqrt(ms1 + RMS_EPS))
        gate = g_ref[0, rows, cols]
        o_ref[0, rows, cols] = (o * inv * gn_ref[...] * (gate * jax.nn.sigmoid(gate))).astype(o_ref.dtype)

    for ch in range(chunks):
        rows = slice(ch * c, (ch + 1) * c)
        states = [start(rows, p) for p in range(HEAD_PAIRS)]
        for lvl in range(_HGRN_LEVELS):
            for s in states:
                level_step(s, lvl)
        for s in states:
            finish(s, rows)


def _hgrn(qh, fh, ih, gh, lb_logits, gn, tri, lvl, layer, tt):
    b, t, _ = qh.shape
    tile = pl.BlockSpec((1, tt, D_HGRN), lambda i, j: (i, j, 0))
    return pl.pallas_call(
        functools.partial(_hgrn_kernel, layer=layer, chunks=tt // HGRN_CHUNK),
        grid=(b, t // tt),
        in_specs=[tile, tile, tile, tile, _resident(lb_logits.shape),
                  _resident(gn.shape), _resident(tri.shape), _resident(lvl.shape)],
        out_specs=tile,
        out_shape=jax.ShapeDtypeStruct((b, t, D_HGRN), BF16),
        scratch_shapes=[pltpu.VMEM((HEAD_PAIRS, LANES, LANES), F32)],
        compiler_params=_compiler_params(("parallel", "arbitrary")),
        name="hgrn2",
    )(qh, fh, ih, gh, lb_logits, gn, tri, lvl)


_FFN_COLS = 256


def _ffn_kernel(x_ref, att_ref, cnv_ref, hg_ref, wo_ref, g_ref, wi_ref, w2_ref, gf_ref, o_ref, a_ref,
                *, d_ff, final):
    mix = jnp.concatenate([att_ref[...], cnv_ref[...], hg_ref[...]], axis=1)
    x1 = x_ref[...] + _dot(mix, wo_ref[...])
    h = _rmsnorm_rows(x1, g_ref[...]).astype(BF16)
    for c0 in range(0, d_ff, _FFN_COLS):
        gate = _dot(h, wi_ref[:, c0:c0 + _FFN_COLS])
        up = _dot(h, wi_ref[:, d_ff + c0:d_ff + c0 + _FFN_COLS])
        a_ref[:, c0:c0 + _FFN_COLS] = (gate * jax.nn.sigmoid(gate) * up).astype(BF16)
    y = x1 + _dot(a_ref[...], w2_ref[...])
    if final:
        y = _rmsnorm_rows(y, gf_ref[...])
    o_ref[...] = y


def _out_ffn(x2d, att, cnv, hg, wo, g, wi, w2, gf, layer, tm, final):
    m, d = x2d.shape
    d_ff = w2.shape[1]
    row = lambda n: pl.BlockSpec((tm, n), lambda i: (i, 0))
    return pl.pallas_call(
        functools.partial(_ffn_kernel, d_ff=d_ff, final=final),
        grid=(m // tm,),
        in_specs=[row(d), row(D_ATT), row(D_CONV), row(D_HGRN), _layer_weight(wo, layer), _resident(g.shape),
                  _layer_weight(wi, layer), _layer_weight(w2, layer), _resident(gf.shape)],
        out_specs=row(d),
        out_shape=jax.ShapeDtypeStruct((m, d), F32),
        scratch_shapes=[pltpu.VMEM((tm, d_ff), BF16)],
        compiler_params=_compiler_params(("parallel",)),
        name="out_ffn",
    )(x2d, att, cnv, hg, wo, g, wi, w2, gf)


def _tiles(t):
    return dict(tm=min(512, t), tt=min(256, t))


_W_IN_SIZES = (D_ATT, D_ATT, D_ATT, ATT_HEADS, 2 * D_CONV, D_HGRN, D_HGRN, D_HGRN, D_HGRN)
_W_IN_AFTER_F = 3 * D_ATT + ATT_HEADS
_W_IN_TAIL = sum(_W_IN_SIZES[4:])


def _w_in_layout_kernel(w_ref, o_ref):
    x = w_ref[0]
    lane = lax.broadcasted_iota(jnp.int32, (1, LANES), 1)
    zero = jnp.zeros((x.shape[0], LANES), F32)

    f_tile = jnp.where(lane < ATT_HEADS, x[:, 3 * D_ATT:3 * D_ATT + LANES], zero)
    shift = _W_IN_AFTER_F % LANES
    tail0 = _W_IN_AFTER_F - shift
    tail = pltpu.roll(x, x.shape[1] - shift, axis=1)[:, tail0:tail0 + _W_IN_TAIL]
    o_ref[0] = jnp.concatenate([x[:, 2 * D_ATT:3 * D_ATT], f_tile, tail, x[:, :2 * D_ATT]], axis=1).astype(o_ref.dtype)


def _w_in_layout(w_in):
    depth, d, d_in = w_in.shape
    rows = min(256, d)
    width = -(-d_in // LANES) * LANES
    d_out = sum(n for _, n in _IN_COLS)
    return pl.pallas_call(
        _w_in_layout_kernel,
        grid=(depth, d // rows),
        in_specs=[pl.BlockSpec((1, rows, width), lambda l, i: (l, i, 0))],
        out_specs=pl.BlockSpec((1, rows, d_out), lambda l, i: (l, i, 0)),
        out_shape=jax.ShapeDtypeStruct((depth, d, d_out), BF16),
        compiler_params=_compiler_params(("parallel", "parallel")),
        name="w_in_layout",
    )(w_in)


def kernel(x, norm_mix_g, w_in, fgate_b, conv_w, conv_b, conv_ln_g, conv_ln_b, hgrn_lb_logits, hgrn_norm_g,
           w_out, norm_ffn_g, w_ffn_in, w_ffn_out, norm_final_g):
    b, t, d = x.shape
    depth = w_in.shape[0]
    tl = _tiles(t)
    row = lambda a: a.reshape(1, -1).astype(F32)

    tri = jnp.asarray(np.tril(np.ones((tl["tm"], tl["tm"]), np.float32)), BF16)
    sel, one = _bias_lane_constants()
    sel, one = jnp.asarray(sel, BF16), jnp.asarray(one)
    tri_h = jnp.asarray(np.tril(np.ones((HGRN_CHUNK, HGRN_CHUNK), np.float32)), BF16)
    lvl = jnp.asarray(_hgrn_level_map(), BF16)
    lb_logits = hgrn_lb_logits.astype(F32)
    gf = row(norm_final_g)

    w_all = _w_in_layout(w_in)
    w_out16, w_ffn_in16, w_ffn_out16 = w_out.astype(BF16), w_ffn_in.astype(BF16), w_ffn_out.astype(BF16)
    for l in range(depth):
        fb = row(jnp.pad(fgate_b[l], (0, LANES - ATT_HEADS)))
        cw = jnp.broadcast_to(conv_w[l][:, None, :], (CONV_WIDTH, _SUBLANES, D_CONV))
        qx, kx, va, cnv, qh, fh, ih, gh, stats = _inproj(x, row(norm_mix_g[l]), w_all, l, fb, tri, sel, one, cw,
                                                  row(conv_b[l]), row(conv_ln_g[l]), row(conv_ln_b[l]))

        att = _attention(qx, kx, va, stats, tl["tm"])

        gn = row(jnp.tile(hgrn_norm_g[l], LANES // HEAD_DIM))
        hg = _hgrn(qh, fh, ih, gh, lb_logits, gn, tri_h, lvl, l, tl["tt"])

        x = _out_ffn(x.reshape(b * t, d), att.reshape(b * t, -1), cnv.reshape(b * t, -1), hg.reshape(b * t, -1),
                     w_out16, row(norm_ffn_g[l]), w_ffn_in16, w_ffn_out16, gf, l, tl["tm"],
                     final=(l == depth - 1)).reshape(b, t, d)
    return x
```

```python
import functools

import numpy as np
import jax
import jax.numpy as jnp
from jax import lax
from jax.experimental import pallas as pl
from jax.experimental.pallas import tpu as pltpu

F32 = jnp.float32
BF16 = jnp.bfloat16

HEAD_DIM = 64
ATT_HEADS = 6
D_ATT = ATT_HEADS * HEAD_DIM
D_CONV = 256
CONV_WIDTH = 31
HGRN_HEADS = 6
D_HGRN = HGRN_HEADS * HEAD_DIM
RMS_EPS = 1e-6
LN_EPS = 1e-5

LANES = 128
HEAD_PAIRS = D_ATT // LANES
CONV_HALO = 32
VMEM_LIMIT_BYTES = 56 * 1024 * 1024

D_ATT_X = ATT_HEADS * LANES
LOG2E = 1.4426950408889634

_IN_COLS = (("va", D_ATT), ("fa", LANES), ("cin", 2 * D_CONV), ("qh", D_HGRN), ("fh", D_HGRN),
            ("ih", D_HGRN), ("gh", D_HGRN), ("qa", D_ATT), ("ka", D_ATT))
_IN_TILES = tuple((name, i) for name, n in _IN_COLS for i in range(n // LANES))
MXU_COLS = 2 * LANES
_IN_CHUNK_ORDER = (2, 3, 1, 0) + tuple(range(4, len(_IN_TILES) * LANES // MXU_COLS))
_IN_OUTPUTS = (("qx", D_ATT_X, BF16), ("kx", D_ATT_X, BF16), ("va", D_ATT, BF16), ("cnv", D_CONV, BF16),
               ("qh", D_HGRN, F32), ("fh", D_HGRN, F32), ("ih", D_HGRN, BF16), ("gh", D_HGRN, F32))
_N_SPLIT = 3


def _nt_dot(a, b):
    return lax.dot_general(a, b, (((1,), (1,)), ((), ())), preferred_element_type=F32)


def _dot(a, b):
    return jnp.dot(a, b, preferred_element_type=F32)


def _rmsnorm_rows(x, g):
    return x * lax.rsqrt(jnp.mean(x * x, axis=-1, keepdims=True) + RMS_EPS) * g


def _compiler_params(semantics):
    return pltpu.CompilerParams(dimension_semantics=semantics, vmem_limit_bytes=VMEM_LIMIT_BYTES)


def _resident(shape):
    nd = len(shape)
    return pl.BlockSpec(shape, lambda *_: (0,) * nd, pipeline_mode=pl.Buffered(1))


def _layer_weight(w_all, layer):
    return pl.BlockSpec((None,) + w_all.shape[1:], lambda *_: (layer, 0, 0), pipeline_mode=pl.Buffered(1))


def _split_bf16(x, n):
    terms = []
    for _ in range(n - 1):
        t = x.astype(BF16)
        terms.append(t)
        x = x - t.astype(F32)
    terms.append(x.astype(BF16))
    return terms


_BIAS_LANES = 2 * _N_SPLIT
_BIAS_STRIDE = 8


def _bias_lane_constants():
    sel = np.zeros((_N_SPLIT * LANES, 2 * LANES), np.float32)
    one = np.zeros((1, 2 * LANES), np.float32)
    for h in range(ATT_HEADS):
        base = h * _BIAS_STRIDE
        for i in range(_N_SPLIT):
            sel[i * LANES + h, base + i] = 1.0
            one[0, base + _N_SPLIT + i] = 1.0
            one[0, LANES + base + i] = 1.0
            sel[i * LANES + h, LANES + base + _N_SPLIT + i] = -1.0
    return sel, one


_STATS_ROWS = (tuple(("qa", p) for p in range(HEAD_PAIRS)) + tuple(("ka", p) for p in range(HEAD_PAIRS))
               + ("c_first", "c_last"))
_CONV_ROWS = 16
_SUBLANES = 8


def _inproj_kernel(x_ref, g_ref, w_ref, fb_ref, tri_ref, sel_ref, one_ref, cw_ref, cb_ref, lg_ref, lb_ref,
                   qx_ref, kx_ref, va_ref, cnv_ref, qh_ref, fh_ref, ih_ref, gh_ref, stats_ref,
                   carry_ref, halo_ref, hs_ref):
    tm = x_ref.shape[1]

    @pl.when(pl.program_id(1) == 0)
    def _():
        carry_ref[...] = jnp.zeros_like(carry_ref)
        halo_ref[...] = jnp.zeros_like(halo_ref)

    h = _rmsnorm_rows(x_ref[0], g_ref[...]).astype(BF16)
    lane = lax.broadcasted_iota(jnp.int32, (1, LANES), 1)
    direct = {"va": va_ref, "qh": qh_ref, "fh": fh_ref, "ih": ih_ref, "gh": gh_ref}
    cin, bias = {}, {}
    stats = {}
    same_head = (lax.broadcasted_iota(jnp.int32, (LANES, LANES), 0) // HEAD_DIM
                 == lax.broadcasted_iota(jnp.int32, (LANES, LANES), 1) // HEAD_DIM).astype(BF16)

    def forget_bias(z):
        log_f = jax.nn.log_sigmoid(z + fb_ref[...])
        tri = tri_ref[...]
        parts = _dot(tri, jnp.concatenate(_split_bf16(log_f, _N_SPLIT), axis=1))
        c = carry_ref[...] + sum(parts[:, i * LANES:(i + 1) * LANES] for i in range(_N_SPLIT))
        carry_ref[...] = c[tm - 1:tm, :]
        c2 = c * LOG2E
        stats["c_first"], stats["c_last"] = c2[0:1, :], c2[tm - 1:tm, :]
        terms = jnp.concatenate(_split_bf16(c2, _N_SPLIT), axis=1)
        bias["all"] = _dot(terms, sel_ref[...]) + one_ref[...]

    def conv_input():
        half = D_CONV // LANES
        glu = jnp.concatenate([cin[i] * jax.nn.sigmoid(cin[half + i]) for i in range(half)], axis=1)
        hs_ref[0, 0:CONV_HALO, :] = halo_ref[...]
        hs_ref[0, CONV_HALO:CONV_HALO + tm, :] = glu
        halo_ref[...] = glu[tm - CONV_HALO:, :]
        span = tm + CONV_HALO - _SUBLANES
        for s in range(1, _SUBLANES):
            hs_ref[s, 0:span, :] = hs_ref[0, s:s + span, :]

    def conv_rows(r0, after):
        first = CONV_HALO - (CONV_WIDTH - 1)
        acc = jnp.minimum(jnp.abs(after[0:_CONV_ROWS, 0:D_CONV]), 0.0)
        for tap in range(CONV_WIDTH):
            off = first + tap
            base = r0 + off // _SUBLANES * _SUBLANES
            taps = jnp.tile(cw_ref[tap], (_CONV_ROWS // _SUBLANES, 1))
            acc = acc + hs_ref[off % _SUBLANES, base:base + _CONV_ROWS, :] * taps
        acc = acc + cb_ref[...]
        mu = jnp.mean(acc, axis=-1, keepdims=True)
        xc = acc - mu
        var = jnp.mean(xc * xc, axis=-1, keepdims=True)
        y = xc * lax.rsqrt(var + LN_EPS) * lg_ref[...] + lb_ref[...]
        cnv_ref[0, r0:r0 + _CONV_ROWS, :] = (y * jax.nn.sigmoid(y)).astype(cnv_ref.dtype)

    def head_pair(name, z, pair, packed_bias, o_ref):
        norm2 = _dot((z * z).astype(BF16), same_head)
        stats[(name, pair)] = jnp.max(norm2, axis=0, keepdims=True)
        for e in range(2):
            hd = 2 * pair + e
            blk = pltpu.roll(z, HEAD_DIM, axis=1) if e else z
            own_bias = pltpu.roll(packed_bias, HEAD_DIM - hd * _BIAS_STRIDE, axis=1)
            tail = jnp.where(lane < HEAD_DIM + _BIAS_LANES, own_bias, 0.0)
            o_ref[0, :, hd * LANES:(hd + 1) * LANES] = jnp.where(lane < HEAD_DIM, blk, tail).astype(o_ref.dtype)

    def route(name, i, z):
        if name in direct:
            direct[name][0, :, i * LANES:(i + 1) * LANES] = z.astype(direct[name].dtype)
        elif name == "fa":
            forget_bias(z)
        elif name == "cin":
            cin[i] = z
            if len(cin) == 2 * D_CONV // LANES:
                conv_input()
        elif name == "qa":
            head_pair(name, z * (HEAD_DIM ** -0.5 * LOG2E), i, bias["all"][:, :LANES], qx_ref)
        elif name == "ka":
            head_pair(name, z, i, bias["all"][:, LANES:], kx_ref)

    conv_starts = list(range(0, tm, _CONV_ROWS))
    conv_slots = len(_IN_CHUNK_ORDER) - _IN_CHUNK_ORDER.index(3) - 1
    done = 0
    def chunk_dot(chunk):
        return _dot(h, w_ref[:, chunk * MXU_COLS:(chunk + 1) * MXU_COLS])

    z_next = chunk_dot(_IN_CHUNK_ORDER[0])
    for pos, chunk in enumerate(_IN_CHUNK_ORDER):
        z = z_next
        if pos + 1 < len(_IN_CHUNK_ORDER):
            z_next = chunk_dot(_IN_CHUNK_ORDER[pos + 1])
        for e in range(MXU_COLS // LANES):
            name, i = _IN_TILES[chunk * (MXU_COLS // LANES) + e]
            route(name, i, z[:, e * LANES:(e + 1) * LANES])
        slot = pos - (len(_IN_CHUNK_ORDER) - conv_slots)
        if slot >= 0:
            upto = -(-(slot + 1) * len(conv_starts) // conv_slots)
            for r0 in conv_starts[done:upto]:
                conv_rows(r0, z)
            done = upto
    stats_ref[0, 0] = jnp.concatenate([stats[k] for k in _STATS_ROWS], axis=0)


def _inproj(x, g, w_all, layer, fb, tri, sel, one, cw, cb, ln_g, ln_b):
    b, t, d = x.shape
    tm = tri.shape[0]
    tile = lambda n: pl.BlockSpec((1, tm, n), lambda i, j: (i, j, 0))
    consts = (g, w_all, fb, tri, sel, one, cw, cb, ln_g, ln_b)
    return pl.pallas_call(
        _inproj_kernel,
        grid=(b, t // tm),
        in_specs=[tile(d)] + [_layer_weight(a, layer) if a is w_all else _resident(a.shape) for a in consts],
        out_specs=[tile(n) for _, n, _ in _IN_OUTPUTS]
        + [pl.BlockSpec((1, 1, _SUBLANES, LANES), lambda i, j: (i, j, 0, 0))],
        out_shape=[jax.ShapeDtypeStruct((b, t, n), dt) for _, n, dt in _IN_OUTPUTS]
        + [jax.ShapeDtypeStruct((b, t // tm, _SUBLANES, LANES), F32)],
        scratch_shapes=[pltpu.VMEM((1, LANES), F32), pltpu.VMEM((CONV_HALO, D_CONV), F32),
                        pltpu.VMEM((_SUBLANES, tm + CONV_HALO, D_CONV), F32)],
        compiler_params=_compiler_params(("parallel", "arbitrary")),
        name="inproj",
    )(x, *consts)


_EXP2_ZERO = -160.0
_NORM_SLACK = 1.02


def _kept_windows(stats):
    heads = slice(0, ATT_HEADS)

    def norms(name):
        first = _STATS_ROWS.index((name, 0))
        sq = stats[:, :, first:first + HEAD_PAIRS, ::HEAD_DIM]
        return jnp.sqrt(sq.reshape(sq.shape[0], sq.shape[1], ATT_HEADS)) * _NORM_SLACK

    qn, kn = norms("qa"), norms("ka")
    c_first = stats[:, :, _STATS_ROWS.index("c_first"), heads]
    c_last = stats[:, :, _STATS_ROWS.index("c_last"), heads]
    bound = (qn[:, :, None] * kn[:, None, :] + c_first[:, :, None] - c_last[:, None, :]
             + (qn * kn)[:, :, None])
    tile = np.arange(stats.shape[1])
    keep = jnp.any(bound > _EXP2_ZERO, axis=-1) | (tile[:, None] == tile[None, :])
    first_kept = jnp.min(jnp.where(keep, tile[None, None, :], tile[None, :, None]), axis=-1)
    window = jnp.max(tile[None, :] - first_kept) + 1
    return first_kept.astype(jnp.int32), window.astype(jnp.int32).reshape(1)


def _attn_kernel(first_ref, window_ref, q_ref, k_ref, v_ref, o_ref, m_ref, acc_ref, *, tile):
    bb, i, slot = pl.program_id(0), pl.program_id(1), pl.program_id(2)
    j = i - (window_ref[0] - 1) + slot
    first = first_ref[bb, i]
    lane = lax.broadcasted_iota(jnp.int32, (1, LANES), 1)

    @pl.when(j == first)
    def _():
        m_ref[...] = jnp.full_like(m_ref, -jnp.inf)
        acc_ref[...] = jnp.zeros_like(acc_ref)

    def scores(h):
        lanes = slice(h * LANES, (h + 1) * LANES)
        return _nt_dot(q_ref[0, :, lanes], k_ref[0, :, lanes])

    def block(diagonal):
        if diagonal:
            row = lax.broadcasted_iota(jnp.int32, (tile, tile), 0)
            col = lax.broadcasted_iota(jnp.int32, (tile, tile), 1)
            keep = col <= row
        s_next = scores(0)
        for h in range(ATT_HEADS):
            s = s_next
            if h + 1 < ATT_HEADS:
                s_next = scores(h + 1)
            if diagonal:
                s = jnp.where(keep, s, -jnp.inf)
            pair = slice((h // 2) * LANES, (h // 2 + 1) * LANES)
            own = (lane < HEAD_DIM) if h % 2 == 0 else (lane >= HEAD_DIM)
            v2 = v_ref[0, :, pair]
            v1 = jnp.where(own, v2, jnp.ones_like(v2))
            m_prev = m_ref[h]
            m_new = jnp.maximum(m_prev, jnp.max(s, axis=-1, keepdims=True))
            p = jnp.exp2(s - jnp.tile(m_new, (1, tile // LANES)))
            acc_ref[h] = jnp.exp2(m_prev - m_new) * acc_ref[h] + _dot(p.astype(BF16), v1)
            m_ref[h] = m_new

    @pl.when((j >= first) & (j < i))
    def _():
        block(False)

    @pl.when(j == i)
    def _():
        block(True)
        for p in range(HEAD_PAIRS):
            a0, a1 = acc_ref[2 * p], acc_ref[2 * p + 1]
            o0 = a0 / pltpu.roll(a0, HEAD_DIM, axis=1)
            o1 = a1 / pltpu.roll(a1, HEAD_DIM, axis=1)
            o_ref[0, :, p * LANES:(p + 1) * LANES] = jnp.where(lane < HEAD_DIM, o0, o1).astype(o_ref.dtype)


def _attention(qx, kx, va, stats, tile):
    b, t, _ = va.shape
    first_kept, window = _kept_windows(stats)
    q_map = lambda bb, i, s, first, window: (bb, i, 0)
    k_map = lambda bb, i, s, first, window: (bb, jnp.maximum(i - (window[0] - 1) + s, first[bb, i]), 0)
    grid_spec = pltpu.PrefetchScalarGridSpec(
        num_scalar_prefetch=2,
        grid=(b, t // tile, window[0]),
        in_specs=[pl.BlockSpec((1, tile, D_ATT_X), q_map),
                  pl.BlockSpec((1, tile, D_ATT_X), k_map),
                  pl.BlockSpec((1, tile, D_ATT), k_map)],
        out_specs=pl.BlockSpec((1, tile, D_ATT), q_map),
        scratch_shapes=[pltpu.VMEM((ATT_HEADS, tile, LANES), F32), pltpu.VMEM((ATT_HEADS, tile, LANES), F32)],
    )
    return pl.pallas_call(
        functools.partial(_attn_kernel, tile=tile),
        grid_spec=grid_spec,
        out_shape=jax.ShapeDtypeStruct((b, t, D_ATT), BF16),
        compiler_params=_compiler_params(("parallel", "arbitrary", "arbitrary")),
        name="fox_attention",
    )(first_kept, window, qx, kx, va)


HGRN_CHUNK = 128
_HGRN_LEVELS = HGRN_CHUNK.bit_length()


def _hgrn_level_map():
    t = np.arange(HGRN_CHUNK)[:, None]
    s = np.arange(HGRN_CHUNK)[None, :]
    high_bit = np.floor(np.log2(np.maximum(t ^ s, 1))).astype(np.int32)
    level = np.where(t == s, 0, np.where(s < t, high_bit + 1, -1)).astype(np.int32)
    return np.concatenate([level, level], axis=1)


def _hgrn_kernel(q_ref, f_ref, v_ref, g_ref, lbl_ref, gn_ref, tri_ref, lvl_ref, o_ref, st_ref,
                 *, layer, chunks):
    c = HGRN_CHUNK

    @pl.when(pl.program_id(1) == 0)
    def _():
        st_ref[...] = jnp.zeros_like(st_ref)

    z = lbl_ref[...]
    e = jnp.exp(z - jnp.max(z, axis=0, keepdims=True))
    if layer > 0:
        lb_all = jnp.sum(e[1:layer + 1], axis=0, keepdims=True) / jnp.sum(e, axis=0, keepdims=True)
    else:
        lb_all = jnp.zeros((1, D_HGRN), F32)

    lane = lax.broadcasted_iota(jnp.int32, (1, LANES), 1)
    head0 = lane < HEAD_DIM
    row = lax.broadcasted_iota(jnp.int32, (c, LANES), 0)
    level = lvl_ref[...]
    vi = lax.broadcasted_iota(jnp.int32, (LANES, LANES), 0) // HEAD_DIM
    di = lax.broadcasted_iota(jnp.int32, (LANES, LANES), 1) // HEAD_DIM
    same_head = vi == di

    def head_split(x):
        return jnp.where(head0, x, 0.0), jnp.where(head0, 0.0, x)

    def head_stack(x0, x1):
        return jnp.concatenate([x0, x1], axis=0).astype(BF16)

    def start(rows, p):
        cols = slice(p * LANES, (p + 1) * LANES)
        lb = lb_all[:, cols]
        q = q_ref[0, rows, cols]
        v2 = v_ref[0, rows, cols].astype(F32)
        f = lb + (1.0 - lb) * jax.nn.sigmoid(f_ref[0, rows, cols])
        kk = 1.0 - f
        g = jnp.log(f) * LOG2E
        b2 = _dot(tri_ref[...], jnp.concatenate(_split_bf16(g, 2), axis=1))
        b_incl = b2[:, :LANES] + b2[:, LANES:]
        kk_stack = head_stack(*head_split(kk))
        return dict(cols=cols, p=p, q=q, v2=v2, kk=kk, q16=q.astype(BF16), g=g, b_incl=b_incl, kk_stack=kk_stack,
                    a=jnp.zeros((c, 2 * c), BF16), seg_end=b_incl)

    def level_step(s, lvl):
        q16, b_incl = s["q16"], s["b_incl"]
        m = 1 << max(lvl - 1, 0)
        if lvl == 0:
            qt, kt = q16, s["kk_stack"]
        elif lvl == 1:
            qt, kt = q16 * jnp.exp2(s["g"]).astype(BF16), s["kk_stack"]
        else:
            half = m // 2
            seg_end = s["seg_end"]
            seg_end = jnp.where((row & half) == 0, pltpu.roll(seg_end, c - half, axis=0), seg_end)
            prev_end = jnp.where(row >= m, pltpu.roll(seg_end, m, axis=0), 0.0)
            qt = q16 * jnp.exp2(b_incl - prev_end).astype(BF16)
            later = jnp.exp2(seg_end - b_incl).astype(BF16)
            kt = s["kk_stack"] * jnp.concatenate([later, later], axis=0)
            s["seg_end"] = seg_end
        s["a"] = jnp.where(level == lvl, _nt_dot(qt, kt).astype(BF16), s["a"])

    def finish(s, rows):
        q, kk, v2, b_incl, p, cols = s["q"], s["kk"], s["v2"], s["b_incl"], s["p"], s["cols"]
        o = _dot(s["a"], head_stack(*head_split(v2)))
        st = st_ref[p]
        o = o + _nt_dot((q * jnp.exp2(b_incl)).astype(BF16), st.astype(BF16))
        b_end = b_incl[c - 1:c, :]
        upd = _dot(v2.T.astype(BF16), (kk * jnp.exp2(b_end - b_incl)).astype(BF16))
        st_ref[p] = st * jnp.exp2(b_end) + jnp.where(same_head, upd, 0.0)

        o2 = o * o
        ms0 = jnp.sum(jnp.where(head0, o2, 0.0), axis=-1, keepdims=True) * (1.0 / HEAD_DIM)
        ms1 = jnp.sum(jnp.where(head0, 0.0, o2), axis=-1, keepdims=True) * (1.0 / HEAD_DIM)
        inv = jnp.where(head0, lax.rsqrt(ms0 + RMS_EPS), lax.rsqrt(ms1 + RMS_EPS))
        gate = g_ref[0, rows, cols]
        o_ref[0, rows, cols] = (o * inv * gn_ref[...] * (gate * jax.nn.sigmoid(gate))).astype(o_ref.dtype)

    for ch in range(chunks):
        rows = slice(ch * c, (ch + 1) * c)
        states = [start(rows, p) for p in range(HEAD_PAIRS)]
        for lvl in range(_HGRN_LEVELS):
            for s in states:
                level_step(s, lvl)
        for s in states:
            finish(s, rows)


def _hgrn(qh, fh, ih, gh, lb_logits, gn, tri, lvl, layer, tt):
    b, t, _ = qh.shape
    tile = pl.BlockSpec((1, tt, D_HGRN), lambda i, j: (i, j, 0))
    return pl.pallas_call(
        functools.partial(_hgrn_kernel, layer=layer, chunks=tt // HGRN_CHUNK),
        grid=(b, t // tt),
        in_specs=[tile, tile, tile, tile, _resident(lb_logits.shape),
                  _resident(gn.shape), _resident(tri.shape), _resident(lvl.shape)],
        out_specs=tile,
        out_shape=jax.ShapeDtypeStruct((b, t, D_HGRN), BF16),
        scratch_shapes=[pltpu.VMEM((HEAD_PAIRS, LANES, LANES), F32)],
        compiler_params=_compiler_params(("parallel", "arbitrary")),
        name="hgrn2",
    )(qh, fh, ih, gh, lb_logits, gn, tri, lvl)


_FFN_COLS = 256


def _ffn_kernel(x_ref, att_ref, cnv_ref, hg_ref, wo_ref, g_ref, wi_ref, w2_ref, gf_ref, o_ref, a_ref,
                *, d_ff, final):
    mix = jnp.concatenate([att_ref[...], cnv_ref[...], hg_ref[...]], axis=1)
    x1 = x_ref[...] + _dot(mix, wo_ref[...])
    h = _rmsnorm_rows(x1, g_ref[...]).astype(BF16)
    for c0 in range(0, d_ff, _FFN_COLS):
        gate = _dot(h, wi_ref[:, c0:c0 + _FFN_COLS])
        up = _dot(h, wi_ref[:, d_ff + c0:d_ff + c0 + _FFN_COLS])
        a_ref[:, c0:c0 + _FFN_COLS] = (gate * jax.nn.sigmoid(gate) * up).astype(BF16)
    y = x1 + _dot(a_ref[...], w2_ref[...])
    if final:
        y = _rmsnorm_rows(y, gf_ref[...])
    o_ref[...] = y


def _out_ffn(x2d, att, cnv, hg, wo, g, wi, w2, gf, layer, tm, final):
    m, d = x2d.shape
    d_ff = w2.shape[1]
    row = lambda n: pl.BlockSpec((tm, n), lambda i: (i, 0))
    return pl.pallas_call(
        functools.partial(_ffn_kernel, d_ff=d_ff, final=final),
        grid=(m // tm,),
        in_specs=[row(d), row(D_ATT), row(D_CONV), row(D_HGRN), _layer_weight(wo, layer), _resident(g.shape),
                  _layer_weight(wi, layer), _layer_weight(w2, layer), _resident(gf.shape)],
        out_specs=row(d),
        out_shape=jax.ShapeDtypeStruct((m, d), F32),
        scratch_shapes=[pltpu.VMEM((tm, d_ff), BF16)],
        compiler_params=_compiler_params(("parallel",)),
        name="out_ffn",
    )(x2d, att, cnv, hg, wo, g, wi, w2, gf)


def _tiles(t):
    return dict(tm=min(512, t), tt=min(512, t))


_W_IN_SIZES = (D_ATT, D_ATT, D_ATT, ATT_HEADS, 2 * D_CONV, D_HGRN, D_HGRN, D_HGRN, D_HGRN)
_W_IN_AFTER_F = 3 * D_ATT + ATT_HEADS
_W_IN_TAIL = sum(_W_IN_SIZES[4:])


def _w_in_layout_kernel(w_ref, o_ref):
    x = w_ref[0]
    lane = lax.broadcasted_iota(jnp.int32, (1, LANES), 1)
    zero = jnp.zeros((x.shape[0], LANES), F32)

    f_tile = jnp.where(lane < ATT_HEADS, x[:, 3 * D_ATT:3 * D_ATT + LANES], zero)
    shift = _W_IN_AFTER_F % LANES
    tail0 = _W_IN_AFTER_F - shift
    tail = pltpu.roll(x, x.shape[1] - shift, axis=1)[:, tail0:tail0 + _W_IN_TAIL]
    o_ref[0] = jnp.concatenate([x[:, 2 * D_ATT:3 * D_ATT], f_tile, tail, x[:, :2 * D_ATT]], axis=1).astype(o_ref.dtype)


def _w_in_layout(w_in):
    depth, d, d_in = w_in.shape
    rows = min(256, d)
    width = -(-d_in // LANES) * LANES
    d_out = sum(n for _, n in _IN_COLS)
    return pl.pallas_call(
        _w_in_layout_kernel,
        grid=(depth, d // rows),
        in_specs=[pl.BlockSpec((1, rows, width), lambda l, i: (l, i, 0))],
        out_specs=pl.BlockSpec((1, rows, d_out), lambda l, i: (l, i, 0)),
        out_shape=jax.ShapeDtypeStruct((depth, d, d_out), BF16),
        compiler_params=_compiler_params(("parallel", "parallel")),
        name="w_in_layout",
    )(w_in)


def kernel(x, norm_mix_g, w_in, fgate_b, conv_w, conv_b, conv_ln_g, conv_ln_b, hgrn_lb_logits, hgrn_norm_g,
           w_out, norm_ffn_g, w_ffn_in, w_ffn_out, norm_final_g):
    b, t, d = x.shape
    depth = w_in.shape[0]
    tl = _tiles(t)
    row = lambda a: a.reshape(1, -1).astype(F32)

    tri = jnp.asarray(np.tril(np.ones((tl["tm"], tl["tm"]), np.float32)), BF16)
    sel, one = _bias_lane_constants()
    sel, one = jnp.asarray(sel, BF16), jnp.asarray(one)
    tri_h = jnp.asarray(np.tril(np.ones((HGRN_CHUNK, HGRN_CHUNK), np.float32)), BF16)
    lvl = jnp.asarray(_hgrn_level_map(), BF16)
    lb_logits = hgrn_lb_logits.astype(F32)
    gf = row(norm_final_g)

    w_all = _w_in_layout(w_in)
    w_out16, w_ffn_in16, w_ffn_out16 = w_out.astype(BF16), w_ffn_in.astype(BF16), w_ffn_out.astype(BF16)
    for l in range(depth):
        fb = row(jnp.pad(fgate_b[l], (0, LANES - ATT_HEADS)))
        cw = jnp.broadcast_to(conv_w[l][:, None, :], (CONV_WIDTH, _SUBLANES, D_CONV))
        qx, kx, va, cnv, qh, fh, ih, gh, stats = _inproj(x, row(norm_mix_g[l]), w_all, l, fb, tri, sel, one, cw,
                                                  row(conv_b[l]), row(conv_ln_g[l]), row(conv_ln_b[l]))

        att = _attention(qx, kx, va, stats, tl["tm"])

        gn = row(jnp.tile(hgrn_norm_g[l], LANES // HEAD_DIM))
        hg = _hgrn(qh, fh, ih, gh, lb_logits, gn, tri_h, lvl, l, tl["tt"])

        x = _out_ffn(x.reshape(b * t, d), att.reshape(b * t, -1), cnv.reshape(b * t, -1), hg.reshape(b * t, -1),
                     w_out16, row(norm_ffn_g[l]), w_ffn_in16, w_ffn_out16, gf, l, tl["tm"],
                     final=(l == depth - 1)).reshape(b, t, d)
    return x
```

```python
import functools

import numpy as np
import jax
import jax.numpy as jnp
from jax import lax
from jax.experimental import pallas as pl
from jax.experimental.pallas import tpu as pltpu

F32 = jnp.float32
BF16 = jnp.bfloat16

HEAD_DIM = 64
ATT_HEADS = 6
D_ATT = ATT_HEADS * HEAD_DIM
D_CONV = 256
CONV_WIDTH = 31
HGRN_HEADS = 6
D_HGRN = HGRN_HEADS * HEAD_DIM
RMS_EPS = 1e-6
LN_EPS = 1e-5

LANES = 128
HEAD_PAIRS = D_ATT // LANES
CONV_HALO = 32
VMEM_LIMIT_BYTES = 56 * 1024 * 1024

D_ATT_X = ATT_HEADS * LANES
LOG2E = 1.4426950408889634

_IN_COLS = (("va", D_ATT), ("fa", LANES), ("cin", 2 * D_CONV), ("qh", D_HGRN), ("fh", D_HGRN),
            ("ih", D_HGRN), ("gh", D_HGRN), ("qa", D_ATT), ("ka", D_ATT))
_IN_TILES = tuple((name, i) for name, n in _IN_COLS for i in range(n // LANES))
MXU_COLS = 2 * LANES
_IN_CHUNK_ORDER = (2, 3, 1, 0) + tuple(range(4, len(_IN_TILES) * LANES // MXU_COLS))
_IN_OUTPUTS = (("qx", D_ATT_X, BF16), ("kx", D_ATT_X, BF16), ("va", D_ATT, BF16), ("cnv", D_CONV, BF16),
               ("qh", D_HGRN, F32), ("fh", D_HGRN, F32), ("ih", D_HGRN, BF16), ("gh", D_HGRN, F32))
_N_SPLIT = 3


def _nt_dot(a, b):
    return lax.dot_general(a, b, (((1,), (1,)), ((), ())), preferred_element_type=F32)


def _dot(a, b):
    return jnp.dot(a, b, preferred_element_type=F32)


def _rmsnorm_rows(x, g):
    return x * lax.rsqrt(jnp.mean(x * x, axis=-1, keepdims=True) + RMS_EPS) * g


def _compiler_params(semantics):
    return pltpu.CompilerParams(dimension_semantics=semantics, vmem_limit_bytes=VMEM_LIMIT_BYTES)


def _resident(shape):
    nd = len(shape)
    return pl.BlockSpec(shape, lambda *_: (0,) * nd, pipeline_mode=pl.Buffered(1))


def _layer_weight(w_all, layer):
    return pl.BlockSpec((None,) + w_all.shape[1:], lambda *_: (layer, 0, 0), pipeline_mode=pl.Buffered(1))


def _split_bf16(x, n):
    terms = []
    for _ in range(n - 1):
        t = x.astype(BF16)
        terms.append(t)
        x = x - t.astype(F32)
    terms.append(x.astype(BF16))
    return terms


_BIAS_LANES = 2 * _N_SPLIT
_BIAS_STRIDE = 8


def _bias_lane_constants():
    sel = np.zeros((_N_SPLIT * LANES, 2 * LANES), np.float32)
    one = np.zeros((1, 2 * LANES), np.float32)
    for h in range(ATT_HEADS):
        base = h * _BIAS_STRIDE
        for i in range(_N_SPLIT):
            sel[i * LANES + h, base + i] = 1.0
            one[0, base + _N_SPLIT + i] = 1.0
            one[0, LANES + base + i] = 1.0
            sel[i * LANES + h, LANES + base + _N_SPLIT + i] = -1.0
    return sel, one


_STATS_ROWS = (tuple(("qa", p) for p in range(HEAD_PAIRS)) + tuple(("ka", p) for p in range(HEAD_PAIRS))
               + ("c_first", "c_last"))
_IN_LOOKAHEAD = 3
_CONV_ROWS = 16
_SUBLANES = 8


def _inproj_kernel(x_ref, g_ref, w_ref, fb_ref, tri_ref, sel_ref, one_ref, cw_ref, cb_ref, lg_ref, lb_ref,
                   qx_ref, kx_ref, va_ref, cnv_ref, qh_ref, fh_ref, ih_ref, gh_ref, stats_ref,
                   carry_ref, halo_ref, hs_ref):
    tm = x_ref.shape[1]

    @pl.when(pl.program_id(1) == 0)
    def _():
        carry_ref[...] = jnp.zeros_like(carry_ref)
        halo_ref[...] = jnp.zeros_like(halo_ref)

    h = _rmsnorm_rows(x_ref[0], g_ref[...]).astype(BF16)
    lane = lax.broadcasted_iota(jnp.int32, (1, LANES), 1)
    direct = {"va": va_ref, "qh": qh_ref, "fh": fh_ref, "ih": ih_ref, "gh": gh_ref}
    cin, bias = {}, {}
    stats = {}
    same_head = (lax.broadcasted_iota(jnp.int32, (LANES, LANES), 0) // HEAD_DIM
                 == lax.broadcasted_iota(jnp.int32, (LANES, LANES), 1) // HEAD_DIM).astype(BF16)

    def forget_bias(z):
        log_f = jax.nn.log_sigmoid(z + fb_ref[...])
        tri = tri_ref[...]
        parts = _dot(tri, jnp.concatenate(_split_bf16(log_f, _N_SPLIT), axis=1))
        c = carry_ref[...] + sum(parts[:, i * LANES:(i + 1) * LANES] for i in range(_N_SPLIT))
        carry_ref[...] = c[tm - 1:tm, :]
        c2 = c * LOG2E
        stats["c_first"], stats["c_last"] = c2[0:1, :], c2[tm - 1:tm, :]
        terms = jnp.concatenate(_split_bf16(c2, _N_SPLIT), axis=1)
        bias["all"] = _dot(terms, sel_ref[...]) + one_ref[...]

    def conv_input():
        half = D_CONV // LANES
        glu = jnp.concatenate([cin[i] * jax.nn.sigmoid(cin[half + i]) for i in range(half)], axis=1)
        hs_ref[0, 0:CONV_HALO, :] = halo_ref[...]
        hs_ref[0, CONV_HALO:CONV_HALO + tm, :] = glu
        halo_ref[...] = glu[tm - CONV_HALO:, :]
        span = tm + CONV_HALO - _SUBLANES
        for s in range(1, _SUBLANES):
            hs_ref[s, 0:span, :] = hs_ref[0, s:s + span, :]

    def conv_rows(r0, after):
        first = CONV_HALO - (CONV_WIDTH - 1)
        acc = jnp.minimum(jnp.abs(after[0:_CONV_ROWS, 0:D_CONV]), 0.0)
        for tap in range(CONV_WIDTH):
            off = first + tap
            base = r0 + off // _SUBLANES * _SUBLANES
            taps = jnp.tile(cw_ref[tap], (_CONV_ROWS // _SUBLANES, 1))
            acc = acc + hs_ref[off % _SUBLANES, base:base + _CONV_ROWS, :] * taps
        acc = acc + cb_ref[...]
        mu = jnp.mean(acc, axis=-1, keepdims=True)
        xc = acc - mu
        var = jnp.mean(xc * xc, axis=-1, keepdims=True)
        y = xc * lax.rsqrt(var + LN_EPS) * lg_ref[...] + lb_ref[...]
        cnv_ref[0, r0:r0 + _CONV_ROWS, :] = (y * jax.nn.sigmoid(y)).astype(cnv_ref.dtype)

    def head_pair(name, z, pair, packed_bias, o_ref):
        norm2 = _dot((z * z).astype(BF16), same_head)
        stats[(name, pair)] = jnp.max(norm2, axis=0, keepdims=True)
        for e in range(2):
            hd = 2 * pair + e
            blk = pltpu.roll(z, HEAD_DIM, axis=1) if e else z
            own_bias = pltpu.roll(packed_bias, HEAD_DIM - hd * _BIAS_STRIDE, axis=1)
            tail = jnp.where(lane < HEAD_DIM + _BIAS_LANES, own_bias, 0.0)
            o_ref[0, :, hd * LANES:(hd + 1) * LANES] = jnp.where(lane < HEAD_DIM, blk, tail).astype(o_ref.dtype)

    def route(name, i, z):
        if name in direct:
            direct[name][0, :, i * LANES:(i + 1) * LANES] = z.astype(direct[name].dtype)
        elif name == "fa":
            forget_bias(z)
        elif name == "cin":
            cin[i] = z
            if len(cin) == 2 * D_CONV // LANES:
                conv_input()
        elif name == "qa":
            head_pair(name, z * (HEAD_DIM ** -0.5 * LOG2E), i, bias["all"][:, :LANES], qx_ref)
        elif name == "ka":
            head_pair(name, z, i, bias["all"][:, LANES:], kx_ref)

    conv_starts = list(range(0, tm, _CONV_ROWS))
    conv_slots = len(_IN_CHUNK_ORDER) - _IN_CHUNK_ORDER.index(3) - 1
    done = 0
    def chunk_dot(chunk):
        return _dot(h, w_ref[:, chunk * MXU_COLS:(chunk + 1) * MXU_COLS])

    ahead = [chunk_dot(c) for c in _IN_CHUNK_ORDER[:_IN_LOOKAHEAD]]
    for pos, chunk in enumerate(_IN_CHUNK_ORDER):
        z = ahead.pop(0)
        if pos + _IN_LOOKAHEAD < len(_IN_CHUNK_ORDER):
            ahead.append(chunk_dot(_IN_CHUNK_ORDER[pos + _IN_LOOKAHEAD]))
        for e in range(MXU_COLS // LANES):
            name, i = _IN_TILES[chunk * (MXU_COLS // LANES) + e]
            route(name, i, z[:, e * LANES:(e + 1) * LANES])
        slot = pos - (len(_IN_CHUNK_ORDER) - conv_slots)
        if slot >= 0:
            upto = -(-(slot + 1) * len(conv_starts) // conv_slots)
            for r0 in conv_starts[done:upto]:
                conv_rows(r0, z)
            done = upto
    stats_ref[0, 0] = jnp.concatenate([stats[k] for k in _STATS_ROWS], axis=0)


def _inproj(x, g, w_all, layer, fb, tri, sel, one, cw, cb, ln_g, ln_b):
    b, t, d = x.shape
    tm = tri.shape[0]
    tile = lambda n: pl.BlockSpec((1, tm, n), lambda i, j: (i, j, 0))
    consts = (g, w_all, fb, tri, sel, one, cw, cb, ln_g, ln_b)
    return pl.pallas_call(
        _inproj_kernel,
        grid=(b, t // tm),
        in_specs=[tile(d)] + [_layer_weight(a, layer) if a is w_all else _resident(a.shape) for a in consts],
        out_specs=[tile(n) for _, n, _ in _IN_OUTPUTS]
        + [pl.BlockSpec((1, 1, _SUBLANES, LANES), lambda i, j: (i, j, 0, 0))],
        out_shape=[jax.ShapeDtypeStruct((b, t, n), dt) for _, n, dt in _IN_OUTPUTS]
        + [jax.ShapeDtypeStruct((b, t // tm, _SUBLANES, LANES), F32)],
        scratch_shapes=[pltpu.VMEM((1, LANES), F32), pltpu.VMEM((CONV_HALO, D_CONV), F32),
                        pltpu.VMEM((_SUBLANES, tm + CONV_HALO, D_CONV), F32)],
        compiler_params=_compiler_params(("parallel", "arbitrary")),
        name="inproj",
    )(x, *consts)


_ATT_LOOKAHEAD = 2
_EXP2_ZERO = -160.0
_NORM_SLACK = 1.02


def _kept_windows(stats):
    heads = slice(0, ATT_HEADS)

    def norms(name):
        first = _STATS_ROWS.index((name, 0))
        sq = stats[:, :, first:first + HEAD_PAIRS, ::HEAD_DIM]
        return jnp.sqrt(sq.reshape(sq.shape[0], sq.shape[1], ATT_HEADS)) * _NORM_SLACK

    qn, kn = norms("qa"), norms("ka")
    c_first = stats[:, :, _STATS_ROWS.index("c_first"), heads]
    c_last = stats[:, :, _STATS_ROWS.index("c_last"), heads]
    bound = (qn[:, :, None] * kn[:, None, :] + c_first[:, :, None] - c_last[:, None, :]
             + (qn * kn)[:, :, None])
    tile = np.arange(stats.shape[1])
    keep = jnp.any(bound > _EXP2_ZERO, axis=-1) | (tile[:, None] == tile[None, :])
    first_kept = jnp.min(jnp.where(keep, tile[None, None, :], tile[None, :, None]), axis=-1)
    window = jnp.max(tile[None, :] - first_kept) + 1
    return first_kept.astype(jnp.int32), window.astype(jnp.int32).reshape(1)


def _attn_kernel(first_ref, window_ref, q_ref, k_ref, v_ref, o_ref, m_ref, acc_ref, *, tile):
    bb, i, slot = pl.program_id(0), pl.program_id(1), pl.program_id(2)
    j = i - (window_ref[0] - 1) + slot
    first = first_ref[bb, i]
    lane = lax.broadcasted_iota(jnp.int32, (1, LANES), 1)

    @pl.when(j == first)
    def _():
        m_ref[...] = jnp.full_like(m_ref, -jnp.inf)
        acc_ref[...] = jnp.zeros_like(acc_ref)

    def scores(h):
        lanes = slice(h * LANES, (h + 1) * LANES)
        return _nt_dot(q_ref[0, :, lanes], k_ref[0, :, lanes])

    def block(diagonal):
        if diagonal:
            row = lax.broadcasted_iota(jnp.int32, (tile, tile), 0)
            col = lax.broadcasted_iota(jnp.int32, (tile, tile), 1)
            keep = col <= row
        ahead = [scores(h) for h in range(_ATT_LOOKAHEAD)]
        for h in range(ATT_HEADS):
            s = ahead.pop(0)
            if h + _ATT_LOOKAHEAD < ATT_HEADS:
                ahead.append(scores(h + _ATT_LOOKAHEAD))
            if diagonal:
                s = jnp.where(keep, s, -jnp.inf)
            pair = slice((h // 2) * LANES, (h // 2 + 1) * LANES)
            own = (lane < HEAD_DIM) if h % 2 == 0 else (lane >= HEAD_DIM)
            v2 = v_ref[0, :, pair]
            v1 = jnp.where(own, v2, jnp.ones_like(v2))
            m_prev = m_ref[h]
            m_new = jnp.maximum(m_prev, jnp.max(s, axis=-1, keepdims=True))
            p = jnp.exp2(s - jnp.tile(m_new, (1, tile // LANES)))
            acc_ref[h] = jnp.exp2(m_prev - m_new) * acc_ref[h] + _dot(p.astype(BF16), v1)
            m_ref[h] = m_new

    @pl.when((j >= first) & (j < i))
    def _():
        block(False)

    @pl.when(j == i)
    def _():
        block(True)
        for p in range(HEAD_PAIRS):
            a0, a1 = acc_ref[2 * p], acc_ref[2 * p + 1]
            o0 = a0 / pltpu.roll(a0, HEAD_DIM, axis=1)
            o1 = a1 / pltpu.roll(a1, HEAD_DIM, axis=1)
            o_ref[0, :, p * LANES:(p + 1) * LANES] = jnp.where(lane < HEAD_DIM, o0, o1).astype(o_ref.dtype)


def _attention(qx, kx, va, stats, tile):
    b, t, _ = va.shape
    first_kept, window = _kept_windows(stats)
    q_map = lambda bb, i, s, first, window: (bb, i, 0)
    k_map = lambda bb, i, s, first, window: (bb, jnp.maximum(i - (window[0] - 1) + s, first[bb, i]), 0)
    grid_spec = pltpu.PrefetchScalarGridSpec(
        num_scalar_prefetch=2,
        grid=(b, t // tile, window[0]),
        in_specs=[pl.BlockSpec((1, tile, D_ATT_X), q_map),
                  pl.BlockSpec((1, tile, D_ATT_X), k_map),
                  pl.BlockSpec((1, tile, D_ATT), k_map)],
        out_specs=pl.BlockSpec((1, tile, D_ATT), q_map),
        scratch_shapes=[pltpu.VMEM((ATT_HEADS, tile, LANES), F32), pltpu.VMEM((ATT_HEADS, tile, LANES), F32)],
    )
    return pl.pallas_call(
        functools.partial(_attn_kernel, tile=tile),
        grid_spec=grid_spec,
        out_shape=jax.ShapeDtypeStruct((b, t, D_ATT), BF16),
        compiler_params=_compiler_params(("parallel", "arbitrary", "arbitrary")),
        name="fox_attention",
    )(first_kept, window, qx, kx, va)


HGRN_CHUNK = 128
_HGRN_LEVELS = HGRN_CHUNK.bit_length()


def _hgrn_level_map():
    t = np.arange(HGRN_CHUNK)[:, None]
    s = np.arange(HGRN_CHUNK)[None, :]
    high_bit = np.floor(np.log2(np.maximum(t ^ s, 1))).astype(np.int32)
    level = np.where(t == s, 0, np.where(s < t, high_bit + 1, -1)).astype(np.int32)
    return np.concatenate([level, level], axis=1)


def _hgrn_kernel(q_ref, f_ref, v_ref, g_ref, lbl_ref, gn_ref, tri_ref, lvl_ref, o_ref, st_ref,
                 *, layer, chunks):
    c = HGRN_CHUNK

    @pl.when(pl.program_id(1) == 0)
    def _():
        st_ref[...] = jnp.zeros_like(st_ref)

    z = lbl_ref[...]
    e = jnp.exp(z - jnp.max(z, axis=0, keepdims=True))
    if layer > 0:
        lb_all = jnp.sum(e[1:layer + 1], axis=0, keepdims=True) / jnp.sum(e, axis=0, keepdims=True)
    else:
        lb_all = jnp.zeros((1, D_HGRN), F32)

    lane = lax.broadcasted_iota(jnp.int32, (1, LANES), 1)
    head0 = lane < HEAD_DIM
    row = lax.broadcasted_iota(jnp.int32, (c, LANES), 0)
    level = lvl_ref[...]
    vi = lax.broadcasted_iota(jnp.int32, (LANES, LANES), 0) // HEAD_DIM
    di = lax.broadcasted_iota(jnp.int32, (LANES, LANES), 1) // HEAD_DIM
    same_head = vi == di

    def head_split(x):
        return jnp.where(head0, x, 0.0), jnp.where(head0, 0.0, x)

    def head_stack(x0, x1):
        return jnp.concatenate([x0, x1], axis=0).astype(BF16)

    def start(rows, p):
        cols = slice(p * LANES, (p + 1) * LANES)
        lb = lb_all[:, cols]
        q = q_ref[0, rows, cols]
        v2 = v_ref[0, rows, cols].astype(F32)
        f = lb + (1.0 - lb) * jax.nn.sigmoid(f_ref[0, rows, cols])
        kk = 1.0 - f
        g = jnp.log(f) * LOG2E
        b2 = _dot(tri_ref[...], jnp.concatenate(_split_bf16(g, 2), axis=1))
        b_incl = b2[:, :LANES] + b2[:, LANES:]
        kk_stack = head_stack(*head_split(kk))
        return dict(cols=cols, p=p, q=q, v2=v2, kk=kk, q16=q.astype(BF16), g=g, b_incl=b_incl, kk_stack=kk_stack,
                    a=jnp.zeros((c, 2 * c), BF16), seg_end=b_incl)

    def level_step(s, lvl):
        q16, b_incl = s["q16"], s["b_incl"]
        m = 1 << max(lvl - 1, 0)
        if lvl == 0:
            qt, kt = q16, s["kk_stack"]
        elif lvl == 1:
            qt, kt = q16 * jnp.exp2(s["g"]).astype(BF16), s["kk_stack"]
        else:
            half = m // 2
            seg_end = s["seg_end"]
            seg_end = jnp.where((row & half) == 0, pltpu.roll(seg_end, c - half, axis=0), seg_end)
            prev_end = jnp.where(row >= m, pltpu.roll(seg_end, m, axis=0), 0.0)
            qt = q16 * jnp.exp2(b_incl - prev_end).astype(BF16)
            later = jnp.exp2(seg_end - b_incl).astype(BF16)
            kt = s["kk_stack"] * jnp.concatenate([later, later], axis=0)
            s["seg_end"] = seg_end
        s["a"] = jnp.where(level == lvl, _nt_dot(qt, kt).astype(BF16), s["a"])

    def finish(s, rows):
        q, kk, v2, b_incl, p, cols = s["q"], s["kk"], s["v2"], s["b_incl"], s["p"], s["cols"]
        o = _dot(s["a"], head_stack(*head_split(v2)))
        st = st_ref[p]
        o = o + _nt_dot((q * jnp.exp2(b_incl)).astype(BF16), st.astype(BF16))
        b_end = b_incl[c - 1:c, :]
        upd = _dot(v2.T.astype(BF16), (kk * jnp.exp2(b_end - b_incl)).astype(BF16))
        st_ref[p] = st * jnp.exp2(b_end) + jnp.where(same_head, upd, 0.0)

        o2 = o * o
        ms0 = jnp.sum(jnp.where(head0, o2, 0.0), axis=-1, keepdims=True) * (1.0 / HEAD_DIM)
        ms1 = jnp.sum(jnp.where(head0, 0.0, o2), axis=-1, keepdims=True) * (1.0 / HEAD_DIM)
        inv = jnp.where(head0, lax.rsqrt(ms0 + RMS_EPS), lax.rsqrt(ms1 + RMS_EPS))
        gate = g_ref[0, rows, cols]
        o_ref[0, rows, cols] = (o * inv * gn_ref[...] * (gate * jax.nn.sigmoid(gate))).astype(o_ref.dtype)

    for ch in range(chunks):
        rows = slice(ch * c, (ch + 1) * c)
        states = [start(rows, p) for p in range(HEAD_PAIRS)]
        for lvl in range(_HGRN_LEVELS):
            for s in states:
                level_step(s, lvl)
        for s in states:
            finish(s, rows)


def _hgrn(qh, fh, ih, gh, lb_logits, gn, tri, lvl, layer, tt):
    b, t, _ = qh.shape
    tile = pl.BlockSpec((1, tt, D_HGRN), lambda i, j: (i, j, 0))
    return pl.pallas_call(
        functools.partial(_hgrn_kernel, layer=layer, chunks=tt // HGRN_CHUNK),
        grid=(b, t // tt),
        in_specs=[tile, tile, tile, tile, _resident(lb_logits.shape),
                  _resident(gn.shape), _resident(tri.shape), _resident(lvl.shape)],
        out_specs=tile,
        out_shape=jax.ShapeDtypeStruct((b, t, D_HGRN), BF16),
        scratch_shapes=[pltpu.VMEM((HEAD_PAIRS, LANES, LANES), F32)],
        compiler_params=_compiler_params(("parallel", "arbitrary")),
        name="hgrn2",
    )(qh, fh, ih, gh, lb_logits, gn, tri, lvl)


_FFN_COLS = 256


def _ffn_kernel(x_ref, att_ref, cnv_ref, hg_ref, wo_ref, g_ref, wi_ref, w2_ref, gf_ref, o_ref, a_ref,
                *, d_ff, final):
    mix = jnp.concatenate([att_ref[...], cnv_ref[...], hg_ref[...]], axis=1)
    x1 = x_ref[...] + _dot(mix, wo_ref[...])
    h = _rmsnorm_rows(x1, g_ref[...]).astype(BF16)
    for c0 in range(0, d_ff, _FFN_COLS):
        gate = _dot(h, wi_ref[:, c0:c0 + _FFN_COLS])
        up = _dot(h, wi_ref[:, d_ff + c0:d_ff + c0 + _FFN_COLS])
        a_ref[:, c0:c0 + _FFN_COLS] = (gate * jax.nn.sigmoid(gate) * up).astype(BF16)
    y = x1 + _dot(a_ref[...], w2_ref[...])
    if final:
        y = _rmsnorm_rows(y, gf_ref[...])
    o_ref[...] = y


def _out_ffn(x2d, att, cnv, hg, wo, g, wi, w2, gf, layer, tm, final):
    m, d = x2d.shape
    d_ff = w2.shape[1]
    row = lambda n: pl.BlockSpec((tm, n), lambda i: (i, 0))
    return pl.pallas_call(
        functools.partial(_ffn_kernel, d_ff=d_ff, final=final),
        grid=(m // tm,),
        in_specs=[row(d), row(D_ATT), row(D_CONV), row(D_HGRN), _layer_weight(wo, layer), _resident(g.shape),
                  _layer_weight(wi, layer), _layer_weight(w2, layer), _resident(gf.shape)],
        out_specs=row(d),
        out_shape=jax.ShapeDtypeStruct((m, d), F32),
        scratch_shapes=[pltpu.VMEM((tm, d_ff), BF16)],
        compiler_params=_compiler_params(("parallel",)),
        name="out_ffn",
    )(x2d, att, cnv, hg, wo, g, wi, w2, gf)


def _tiles(t):
    return dict(tm=min(512, t), tt=min(512, t))


_W_IN_SIZES = (D_ATT, D_ATT, D_ATT, ATT_HEADS, 2 * D_CONV, D_HGRN, D_HGRN, D_HGRN, D_HGRN)
_W_IN_AFTER_F = 3 * D_ATT + ATT_HEADS
_W_IN_TAIL = sum(_W_IN_SIZES[4:])


def _w_in_layout_kernel(w_ref, o_ref):
    x = w_ref[0]
    lane = lax.broadcasted_iota(jnp.int32, (1, LANES), 1)
    zero = jnp.zeros((x.shape[0], LANES), F32)

    f_tile = jnp.where(lane < ATT_HEADS, x[:, 3 * D_ATT:3 * D_ATT + LANES], zero)
    shift = _W_IN_AFTER_F % LANES
    tail0 = _W_IN_AFTER_F - shift
    tail = pltpu.roll(x, x.shape[1] - shift, axis=1)[:, tail0:tail0 + _W_IN_TAIL]
    o_ref[0] = jnp.concatenate([x[:, 2 * D_ATT:3 * D_ATT], f_tile, tail, x[:, :2 * D_ATT]], axis=1).astype(o_ref.dtype)


def _w_in_layout(w_in):
    depth, d, d_in = w_in.shape
    rows = min(256, d)
    width = -(-d_in // LANES) * LANES
    d_out = sum(n for _, n in _IN_COLS)
    return pl.pallas_call(
        _w_in_layout_kernel,
        grid=(depth, d // rows),
        in_specs=[pl.BlockSpec((1, rows, width), lambda l, i: (l, i, 0))],
        out_specs=pl.BlockSpec((1, rows, d_out), lambda l, i: (l, i, 0)),
        out_shape=jax.ShapeDtypeStruct((depth, d, d_out), BF16),
        compiler_params=_compiler_params(("parallel", "parallel")),
        name="w_in_layout",
    )(w_in)


def kernel(x, norm_mix_g, w_in, fgate_b, conv_w, conv_b, conv_ln_g, conv_ln_b, hgrn_lb_logits, hgrn_norm_g,
           w_out, norm_ffn_g, w_ffn_in, w_ffn_out, norm_final_g):
    b, t, d = x.shape
    depth = w_in.shape[0]
    tl = _tiles(t)
    row = lambda a: a.reshape(1, -1).astype(F32)

    tri = jnp.asarray(np.tril(np.ones((tl["tm"], tl["tm"]), np.float32)), BF16)
    sel, one = _bias_lane_constants()
    sel, one = jnp.asarray(sel, BF16), jnp.asarray(one)
    tri_h = jnp.asarray(np.tril(np.ones((HGRN_CHUNK, HGRN_CHUNK), np.float32)), BF16)
    lvl = jnp.asarray(_hgrn_level_map(), BF16)
    lb_logits = hgrn_lb_logits.astype(F32)
    gf = row(norm_final_g)

    w_all = _w_in_layout(w_in)
    w_out16, w_ffn_in16, w_ffn_out16 = w_out.astype(BF16), w_ffn_in.astype(BF16), w_ffn_out.astype(BF16)
    for l in range(depth):
        fb = row(jnp.pad(fgate_b[l], (0, LANES - ATT_HEADS)))
        cw = jnp.broadcast_to(conv_w[l][:, None, :], (CONV_WIDTH, _SUBLANES, D_CONV))
        qx, kx, va, cnv, qh, fh, ih, gh, stats = _inproj(x, row(norm_mix_g[l]), w_all, l, fb, tri, sel, one, cw,
                                                  row(conv_b[l]), row(conv_ln_g[l]), row(conv_ln_b[l]))

        att = _attention(qx, kx, va, stats, tl["tm"])

        gn = row(jnp.tile(hgrn_norm_g[l], LANES // HEAD_DIM))
        hg = _hgrn(qh, fh, ih, gh, lb_logits, gn, tri_h, lvl, l, tl["tt"])

        x = _out_ffn(x.reshape(b * t, d), att.reshape(b * t, -1), cnv.reshape(b * t, -1), hg.reshape(b * t, -1),
                     w_out16, row(norm_ffn_g[l]), w_ffn_in16, w_ffn_out16, gf, l, tl["tm"],
                     final=(l == depth - 1)).reshape(b, t, d)
    return x
```
